```python
import jax, jax.numpy as jnp
from jax import lax
import numpy as np

D_MODEL = 1024
BATCH = 16
SEQ = 256
DEPTH = 2
DEC_BATCH = 2
DEC_SEQ = 4096
PAST_LEN = 512

GRID_W = 64
HEAD_DIM = 64
N_HEADS_A = 16
N_KV_A = 4
WINDOW = 128
BLOCK = 128
N_HEADS_B = 16
WIN_R = 8
WIN_C = 16
D_FF = 2816
CONV_W = 3
ROPE_BASE = 10000.0
EPS = 1e-6
SCALE = HEAD_DIM ** -0.5
N_A = (DEPTH + 1) // 2
N_B = DEPTH // 2

kernel_name = 'hybrid_prefix_diffusion_step'


def rmsnorm(x, w):
    xf = x.astype(jnp.float32)
    y = xf * lax.rsqrt(jnp.mean(xf * xf, axis=-1, keepdims=True) + EPS)
    return (y * w.astype(jnp.float32)).astype(x.dtype)


def adaln(cond, w_ada, b_ada):
    m = jax.nn.silu(cond) @ w_ada + b_ada
    return [t[:, None, :] for t in jnp.split(m, 6, axis=-1)]


def modulate(h, shift, scale):
    return h * (1 + scale) + shift


def project_qkv(h, w_qkv, qn, kn, n_heads, n_kv):
    b, l, _ = h.shape
    qkv = h @ w_qkv
    q, k, v = jnp.split(qkv, [n_heads * HEAD_DIM, (n_heads + n_kv) * HEAD_DIM], axis=-1)
    q = rmsnorm(q.reshape(b, l, n_heads, HEAD_DIM), qn)
    k = rmsnorm(k.reshape(b, l, n_kv, HEAD_DIM), kn)
    v = v.reshape(b, l, n_kv, HEAD_DIM)
    return q, k, v


def rope_1d(x, pos):
    half = x.shape[-1] // 2
    freqs = ROPE_BASE ** (-jnp.arange(half, dtype=jnp.float32) / half)
    ang = pos.astype(jnp.float32)[:, None] * freqs[None, :]
    cos = jnp.cos(ang)[:, None, :]
    sin = jnp.sin(ang)[:, None, :]
    xf = x.astype(jnp.float32)
    x1, x2 = xf[..., :half], xf[..., half:]
    return jnp.concatenate([x1 * cos - x2 * sin, x2 * cos + x1 * sin], axis=-1).astype(x.dtype)


def rope_2d(x):
    t = jnp.arange(x.shape[1])
    half = HEAD_DIM // 2
    return jnp.concatenate([rope_1d(x[..., :half], t // GRID_W), rope_1d(x[..., half:], t % GRID_W)], axis=-1)


def context_attention(q, k, v, sink):
    b, lc, h, _ = q.shape
    n_kv = k.shape[2]
    g = h // n_kv
    qg = q.reshape(b, lc, n_kv, g, HEAD_DIM)
    s = jnp.einsum('bqkgd,bckd->bkgqc', qg, k).astype(jnp.float32) * SCALE
    if sink is not None:
        sk = jnp.broadcast_to(sink.astype(jnp.float32).reshape(n_kv, g)[None, :, :, None, None], s.shape[:-1] + (1,))
        s = jnp.concatenate([s, sk], axis=-1)
    p = jax.nn.softmax(s, axis=-1)[..., :lc].astype(v.dtype)
    o = jnp.einsum('bkgqc,bckd->bqkgd', p, v)
    return o.reshape(b, lc, h * HEAD_DIM)


def window_attention_latent(q, k, v, kc, vc, sink):
    b, l, h, _ = q.shape
    n_kv = k.shape[2]
    g = h // n_kv
    lc = kc.shape[1]
    nb = l // BLOCK
    qb = q.reshape(b, nb, BLOCK, n_kv, g, HEAD_DIM)
    pad = ((0, 0), (BLOCK, BLOCK), (0, 0), (0, 0))
    kp = jnp.pad(k, pad).reshape(b, nb + 2, BLOCK, n_kv, HEAD_DIM)
    vp = jnp.pad(v, pad).reshape(b, nb + 2, BLOCK, n_kv, HEAD_DIM)
    kb = jnp.concatenate([kp[:, :-2], kp[:, 1:-1], kp[:, 2:]], axis=2)
    vb = jnp.concatenate([vp[:, :-2], vp[:, 1:-1], vp[:, 2:]], axis=2)
    qpos = jnp.arange(nb)[:, None] * BLOCK + jnp.arange(BLOCK)[None, :]
    kpos = jnp.arange(nb)[:, None] * BLOCK - BLOCK + jnp.arange(3 * BLOCK)[None, :]
    valid = (jnp.abs(qpos[:, :, None] - kpos[:, None, :]) <= WINDOW) & (kpos[:, None, :] >= 0) & (kpos[:, None, :] < l)
    s_loc = jnp.einsum('bnqkgd,bnjkd->bkgnqj', qb, kb).astype(jnp.float32) * SCALE
    s_loc = jnp.where(valid[None, None, None], s_loc, -jnp.inf)
    s_ctx = jnp.einsum('bnqkgd,bckd->bkgnqc', qb, kc).astype(jnp.float32) * SCALE
    s_sink = jnp.broadcast_to(sink.astype(jnp.float32).reshape(n_kv, g)[None, :, :, None, None, None], s_loc.shape[:-1] + (1,))
    p = jax.nn.softmax(jnp.concatenate([s_loc, s_ctx, s_sink], axis=-1), axis=-1)
    p_loc = p[..., :3 * BLOCK].astype(v.dtype)
    p_ctx = p[..., 3 * BLOCK:3 * BLOCK + lc].astype(v.dtype)
    o = jnp.einsum('bkgnqj,bnjkd->bnqkgd', p_loc, vb) + jnp.einsum('bkgnqc,bckd->bnqkgd', p_ctx, vc)
    return o.reshape(b, l, h * HEAD_DIM)


def neighborhood_attention_latent(q, k, v, kc, vc, rpb):
    b, l, h, _ = q.shape
    rows = l // GRID_W
    wr = min(WIN_R, rows)
    r = jnp.arange(rows)
    rs = jnp.clip(r - wr // 2, 0, rows - wr)
    key_rows = rs[:, None] + jnp.arange(wr)[None, :]
    qg = q.reshape(b, rows, GRID_W, h, HEAD_DIM)
    kg = k.reshape(b, rows, GRID_W, h, HEAD_DIM)[:, key_rows]
    vg = v.reshape(b, rows, GRID_W, h, HEAD_DIM)[:, key_rows]
    col = jnp.arange(GRID_W)
    cs = jnp.clip(col - WIN_C // 2, 0, GRID_W - WIN_C)
    col_ok = (col[None, :] >= cs[:, None]) & (col[None, :] < cs[:, None] + WIN_C)
    dr = key_rows - r[:, None]
    dc = jnp.clip(col[None, :] - col[:, None], -(WIN_C - 1), WIN_C - 1)
    bias = rpb.astype(jnp.float32)[:, dr[:, None, :, None] + WIN_R - 1, dc[None, :, None, :] + WIN_C - 1]
    s_loc = jnp.einsum('brqhd,brikhd->bhrqik', qg, kg).astype(jnp.float32) * SCALE + bias[None]
    s_loc = jnp.where(col_ok[None, None, None, :, None, :], s_loc, -jnp.inf)
    s_loc = s_loc.reshape(b, h, rows, GRID_W, wr * GRID_W)
    s_ctx = jnp.einsum('brqhd,bchd->bhrqc', qg, kc).astype(jnp.float32) * SCALE
    p = jax.nn.softmax(jnp.concatenate([s_loc, s_ctx], axis=-1), axis=-1)
    p_loc = p[..., :wr * GRID_W].astype(v.dtype)
    p_ctx = p[..., wr * GRID_W:].astype(v.dtype)
    o = jnp.einsum('bhrqj,brjhd->brqhd', p_loc, vg.reshape(b, rows, wr * GRID_W, h, HEAD_DIM))
    o = o + jnp.einsum('bhrqc,bchd->brqhd', p_ctx, vc)
    return o.reshape(b, l, h * HEAD_DIM)


def conv_ffn(h, w_up, conv_w, conv_b, w_down):
    l = h.shape[1]
    u = h @ w_up
    half = CONV_W // 2
    up = jnp.pad(u, ((0, 0), (half, half), (0, 0)))
    u = sum(up[:, o:o + l] * conv_w[o] for o in range(CONV_W)) + conv_b
    gate, val = jnp.split(u, 2, axis=-1)
    return (jax.nn.silu(gate) * val) @ w_down


def setup_inputs(seed: int = 0) -> dict:
    key = jax.random.key(seed)
    ks = jax.random.split(key, 26)
    f32 = jnp.float32

    def nrm(k, shape, scale):
        return jax.random.normal(k, shape, f32) * scale

    qkv_a = (N_HEADS_A + 2 * N_KV_A) * HEAD_DIM
    qkv_b = 3 * N_HEADS_B * HEAD_DIM
    return {
        'x_prompt': nrm(ks[0], (BATCH, SEQ, D_MODEL), 1.0),
        'x_sample': nrm(ks[1], (DEC_BATCH, DEC_SEQ, D_MODEL), 1.0),
        'cache_k_a': nrm(ks[2], (DEC_BATCH, N_A, PAST_LEN, N_KV_A, HEAD_DIM), 1.0),
        'cache_v_a': nrm(ks[3], (DEC_BATCH, N_A, PAST_LEN, N_KV_A, HEAD_DIM), 1.0),
        'cache_k_b': nrm(ks[4], (DEC_BATCH, N_B, PAST_LEN, N_HEADS_B, HEAD_DIM), 1.0),
        'cache_v_b': nrm(ks[5], (DEC_BATCH, N_B, PAST_LEN, N_HEADS_B, HEAD_DIM), 1.0),
        'c': nrm(ks[6], (DEC_BATCH, D_MODEL), 1.0),
        'c_ctx': nrm(ks[7], (D_MODEL,), 1.0),
        'norm_attn_w': 1.0 + nrm(ks[8], (DEPTH, D_MODEL), 0.05),
        'norm_ffn_w': 1.0 + nrm(ks[9], (DEPTH, D_MODEL), 0.05),
        'w_ada': nrm(ks[10], (DEPTH, D_MODEL, 6 * D_MODEL), D_MODEL ** -0.5),
        'b_ada': nrm(ks[11], (DEPTH, 6 * D_MODEL), 0.02),
        'w_qkv_a': nrm(ks[12], (N_A, D_MODEL, qkv_a), D_MODEL ** -0.5),
        'q_norm_a': 1.0 + nrm(ks[13], (N_A, HEAD_DIM), 0.05),
        'k_norm_a': 1.0 + nrm(ks[14], (N_A, HEAD_DIM), 0.05),
        'sink_a': nrm(ks[15], (N_A, N_HEADS_A), 0.5),
        'w_o_a': nrm(ks[16], (N_A, N_HEADS_A * HEAD_DIM, D_MODEL), (N_HEADS_A * HEAD_DIM) ** -0.5),
        'w_qkv_b': nrm(ks[17], (N_B, D_MODEL, qkv_b), D_MODEL ** -0.5),
        'q_norm_b': 1.0 + nrm(ks[18], (N_B, HEAD_DIM), 0.05),
        'k_norm_b': 1.0 + nrm(ks[19], (N_B, HEAD_DIM), 0.05),
        'rpb_b': nrm(ks[20], (N_B, N_HEADS_B, 2 * WIN_R - 1, 2 * WIN_C - 1), 0.5),
        'w_o_b': nrm(ks[21], (N_B, N_HEADS_B * HEAD_DIM, D_MODEL), (N_HEADS_B * HEAD_DIM) ** -0.5),
        'w_up': nrm(ks[22], (DEPTH, D_MODEL, 2 * D_FF), D_MODEL ** -0.5),
        'conv_w': nrm(ks[23], (DEPTH, CONV_W, 2 * D_FF), 0.5),
        'conv_b': nrm(ks[24], (DEPTH, 2 * D_FF), 0.02),
        'w_down': nrm(ks[25], (DEPTH, D_FF, D_MODEL), D_FF ** -0.5),
    }


def reference(x_prompt, x_sample, cache_k_a, cache_v_a, cache_k_b, cache_v_b, c, c_ctx,
              norm_attn_w, norm_ffn_w, w_ada, b_ada,
              w_qkv_a, q_norm_a, k_norm_a, sink_a, w_o_a,
              w_qkv_b, q_norm_b, k_norm_b, rpb_b, w_o_b,
              w_up, conv_w, conv_b, w_down):
    xp = x_prompt
    xs = x_sample
    new_k_a, new_v_a, new_k_b, new_v_b = [], [], [], []
    for i in range(DEPTH):
        j = i // 2
        mp = adaln(c_ctx[None, :], w_ada[i], b_ada[i])
        ms = adaln(c, w_ada[i], b_ada[i])
        hp = modulate(rmsnorm(xp, norm_attn_w[i]), mp[0], mp[1])
        hs = modulate(rmsnorm(xs, norm_attn_w[i]), ms[0], ms[1])
        if i % 2 == 0:
            q, k, v = project_qkv(hp, w_qkv_a[j], q_norm_a[j], k_norm_a[j], N_HEADS_A, N_KV_A)
            op = context_attention(q, k, v, sink_a[j]) @ w_o_a[j]
            new_k_a.append(k)
            new_v_a.append(v)
            q, k, v = project_qkv(hs, w_qkv_a[j], q_norm_a[j], k_norm_a[j], N_HEADS_A, N_KV_A)
            o_s = window_attention_latent(rope_2d(q), rope_2d(k), v, cache_k_a[:, j], cache_v_a[:, j], sink_a[j]) @ w_o_a[j]
        else:
            q, k, v = project_qkv(hp, w_qkv_b[j], q_norm_b[j], k_norm_b[j], N_HEADS_B, N_HEADS_B)
            op = context_attention(q, k, v, None) @ w_o_b[j]
            new_k_b.append(k)
            new_v_b.append(v)
            q, k, v = project_qkv(hs, w_qkv_b[j], q_norm_b[j], k_norm_b[j], N_HEADS_B, N_HEADS_B)
            o_s = neighborhood_attention_latent(q, k, v, cache_k_b[:, j], cache_v_b[:, j], rpb_b[j]) @ w_o_b[j]
        xp = xp + mp[2] * op
        xs = xs + ms[2] * o_s
        hp = modulate(rmsnorm(xp, norm_ffn_w[i]), mp[3], mp[4])
        hs = modulate(rmsnorm(xs, norm_ffn_w[i]), ms[3], ms[4])
        xp = xp + mp[5] * conv_ffn(hp, w_up[i], conv_w[i], conv_b[i], w_down[i])
        xs = xs + ms[5] * conv_ffn(hs, w_up[i], conv_w[i], conv_b[i], w_down[i])
    return (xp, xs, jnp.stack(new_k_a, axis=1), jnp.stack(new_v_a, axis=1), jnp.stack(new_k_b, axis=1), jnp.stack(new_v_b, axis=1))
```

```python
import functools

import jax
import jax.numpy as jnp
from jax import lax
from jax.experimental import pallas as pl
from jax.experimental.pallas import tpu as pltpu

D_MODEL = 1024
HEAD_DIM = 64
N_HEADS = 16
N_KV_A = 4
GRID_W = 64
WINDOW = 128
WIN_R = 8
WIN_C = 16
D_FF = 2816
ROPE_BASE = 10000.0
EPS = 1e-6
SCALE = HEAD_DIM ** -0.5

LANES = 128
HALO = 8
NORM_CHUNK = 256
FF_CHUNK = 256
NEG = -1e30
N_MOD_ROWS = 8

F32 = jnp.float32
BF16 = jnp.bfloat16
VMEM_LIMIT = 56 * 1024 * 1024


def _cparams(n_axes):
    return pltpu.CompilerParams(
        dimension_semantics=("arbitrary",) * n_axes, vmem_limit_bytes=VMEM_LIMIT)


def _silu(x):
    return x * (1.0 / (1.0 + jnp.exp(-x)))


def _rms_modulate(x, nw, shift, scale):
    ms = jnp.mean(x * x, axis=-1, keepdims=True)
    y = x * lax.rsqrt(ms + EPS) * nw
    return y * (1.0 + scale) + shift


def _ada_kernel(c_ref, w_ref, b_ref, o_ref):
    a = _silu(c_ref[...])
    o_ref[...] = jnp.dot(a, w_ref[...], precision=lax.Precision.HIGHEST,
                         preferred_element_type=F32) + b_ref[...]


def _ada_call(cond, w_ada, b_ada):
    depth = w_ada.shape[0]
    tn = 1024
    return pl.pallas_call(
        _ada_kernel,
        grid=(depth, 6 * D_MODEL // tn),
        in_specs=[
            pl.BlockSpec((N_MOD_ROWS, D_MODEL), lambda l, n: (0, 0)),
            pl.BlockSpec((None, D_MODEL, tn), lambda l, n: (l, 0, n)),
            pl.BlockSpec((None, 1, tn), lambda l, n: (l, 0, n)),
        ],
        out_specs=pl.BlockSpec((None, N_MOD_ROWS, tn), lambda l, n: (l, 0, n)),
        out_shape=jax.ShapeDtypeStruct((depth, N_MOD_ROWS, 6 * D_MODEL), F32),
        compiler_params=_cparams(2),
        name="adaln",
    )(cond, w_ada, b_ada)


def _head_rms(t, bd, w):
    outs = []
    for c in range(t.shape[1] // NORM_CHUNK):
        tc = t[:, c * NORM_CHUNK:(c + 1) * NORM_CHUNK]
        ms = jnp.dot((tc * tc).astype(BF16), bd, preferred_element_type=F32)
        outs.append(tc * lax.rsqrt(ms + EPS))
    return jnp.concatenate(outs, axis=1) * w


def _rope(t, cos, sin_signed, lo16):
    outs = []
    for c in range(t.shape[1] // LANES):
        tc = t[:, c * LANES:(c + 1) * LANES]
        partner = jnp.where(lo16, pltpu.roll(tc, LANES - 16, axis=1), pltpu.roll(tc, 16, axis=1))
        outs.append(tc * cos + partner * sin_signed)
    return jnp.concatenate(outs, axis=1)


def _dup_heads(t, lo64):
    outs = []
    for c in range(t.shape[1] // LANES):
        tc = t[:, c * LANES:(c + 1) * LANES]
        sw = pltpu.roll(tc, HEAD_DIM, axis=1)
        outs.append(jnp.where(lo64, tc, sw))
        outs.append(jnp.where(lo64, sw, tc))
    return jnp.concatenate(outs, axis=1)


def _qkv_kernel(*refs, nkv, rope, dup, f32_kv):
    x_ref, mod_ref, nw_ref, w_ref, qn_ref, kn_ref, bd_ref = refs[:7]
    pos = 7
    if rope:
        cos_ref, sin_ref = refs[pos:pos + 2]
        pos += 2
    outs = refs[pos:]
    nq = N_HEADS * HEAD_DIM
    lane = lax.broadcasted_iota(jnp.int32, (1, LANES), 1)

    h = _rms_modulate(x_ref[...], nw_ref[...], mod_ref[:, 0:D_MODEL],
                      mod_ref[:, D_MODEL:2 * D_MODEL])
    qkv = jnp.dot(h.astype(BF16), w_ref[...], preferred_element_type=F32)
    bd = bd_ref[...]
    q = _head_rms(qkv[:, :nq], bd, qn_ref[...])
    k = _head_rms(qkv[:, nq:nq + nkv], bd, kn_ref[...])
    v = qkv[:, nq + nkv:]
    if rope:
        lo16 = (lane & 16) == 0
        q = _rope(q, cos_ref[...], sin_ref[...], lo16)
        k = _rope(k, cos_ref[...], sin_ref[...], lo16)
    outs[0][...] = (q * SCALE).astype(BF16)
    o = 1
    if f32_kv:
        outs[o][...] = k
        outs[o + 1][...] = v
        o += 2
    if dup:
        lo64 = lane < HEAD_DIM
        k = _dup_heads(k, lo64)
        v = _dup_heads(v, lo64)
    if dup or not f32_kv:
        outs[o][...] = k.astype(BF16)
        outs[o + 1][...] = v.astype(BF16)


def _qkv_call(x, mod4, layer, row0, nw, w, qn, kn, bd, rope_tabs, *, nkv, dup, f32_kv, tm):
    g_n, l_n, _ = x.shape
    rope = rope_tabs is not None
    nq = N_HEADS * HEAD_DIM
    xmap = lambda g, i: (g, i, 0)
    const = lambda g, i: (0, 0)
    in_specs = [
        pl.BlockSpec((None, tm, D_MODEL), xmap),
        pl.BlockSpec((None, None, 1, 6 * D_MODEL), lambda g, i: (layer, row0 + g, 0, 0)),
        pl.BlockSpec((1, D_MODEL), const),
        pl.BlockSpec(w.shape, const),
        pl.BlockSpec((1, nq), const),
        pl.BlockSpec((1, nkv), const),
        pl.BlockSpec((NORM_CHUNK, NORM_CHUNK), const),
    ]
    args = [x, mod4, nw, w, qn, kn, bd]
    if rope:
        in_specs += [pl.BlockSpec((tm, LANES), lambda g, i: (i, 0))] * 2
        args += list(rope_tabs)
    out_shape = [jax.ShapeDtypeStruct((g_n, l_n, nq), BF16)]
    out_specs = [pl.BlockSpec((None, tm, nq), xmap)]
    if f32_kv:
        out_shape += [jax.ShapeDtypeStruct((g_n, l_n, nkv), F32)] * 2
        out_specs += [pl.BlockSpec((None, tm, nkv), xmap)] * 2
    if dup or not f32_kv:
        wkv = nkv * 2 if dup else nkv
        out_shape += [jax.ShapeDtypeStruct((g_n, l_n, wkv), BF16)] * 2
        out_specs += [pl.BlockSpec((None, tm, wkv), xmap)] * 2
    return pl.pallas_call(
        functools.partial(_qkv_kernel, nkv=nkv, rope=rope, dup=dup, f32_kv=f32_kv),
        grid=(g_n, l_n // tm),
        in_specs=in_specs,
        out_specs=out_specs,
        out_shape=out_shape,
        compiler_params=_cparams(2),
        name="qkv",
    )(*args)


def _attend(q2_list, segs, sinks):
    m_rows = q2_list[0].shape[0]
    lo = lax.broadcasted_iota(jnp.int32, (1, LANES), 1) < HEAD_DIM
    zero = jnp.zeros((), BF16)
    stacked = []
    for q2 in q2_list:
        stacked.append(jnp.where(lo, q2, zero))
        stacked.append(jnp.where(lo, zero, q2))
    qs = jnp.concatenate(stacked, axis=0)
    scores = []
    for k2, _, bias in segs:
        s = lax.dot_general(qs, k2, (((1,), (1,)), ((), ())), preferred_element_type=F32)
        scores.append(s if bias is None else s + bias)
    mx = functools.reduce(jnp.maximum, [jnp.max(s, axis=-1, keepdims=True) for s in scores])
    if sinks is not None:
        sink_col = jnp.concatenate(
            [jnp.full((m_rows, 1), sk, F32) for sk in sinks], axis=0)
        mx = jnp.maximum(mx, sink_col)
    den = jnp.exp(sink_col - mx) if sinks is not None else jnp.zeros_like(mx)
    acc = jnp.zeros((qs.shape[0], LANES), F32)
    for s, (_, v2, _) in zip(scores, segs):
        p = jnp.exp(s - mx)
        den = den + jnp.sum(p, axis=-1, keepdims=True)
        acc = acc + jnp.dot(p.astype(BF16), v2, preferred_element_type=F32)
    acc = acc / den
    return [jnp.where(lo, acc[(2 * i) * m_rows:(2 * i + 1) * m_rows],
                      acc[(2 * i + 1) * m_rows:(2 * i + 2) * m_rows])
            for i in range(len(q2_list))]


def _dup_f32_to_bf16(t):
    lo64 = lax.broadcasted_iota(jnp.int32, (1, LANES), 1) < HEAD_DIM
    return _dup_heads(t, lo64).astype(BF16)


def _gqa_heads(q, segs_of_group, sink_ref, o_ref):
    for g in range(N_KV_A):
        q2s = [q[:, (2 * g + i) * LANES:(2 * g + i + 1) * LANES] for i in range(2)]
        sinks = [sink_ref[4 * g + i] for i in range(4)]
        outs = _attend(q2s, segs_of_group(g), sinks)
        for i in range(2):
            o_ref[:, (2 * g + i) * LANES:(2 * g + i + 1) * LANES] = outs[i].astype(o_ref.dtype)


def _ctx_attn_a_kernel(sink_ref, q_ref, k_ref, v_ref, o_ref):
    q = q_ref[...]
    k = k_ref[...]
    v = v_ref[...]
    tile = lambda t, g: t[:, g * LANES:(g + 1) * LANES]
    _gqa_heads(q, lambda g: [(tile(k, g), tile(v, g), None)], sink_ref, o_ref)


def _ctx_attn_b_kernel(q_ref, k_ref, v_ref, o_ref):
    q = q_ref[...]
    k = k_ref[...].astype(BF16)
    v = v_ref[...].astype(BF16)
    for j in range(N_HEADS // 2):
        sl = slice(j * LANES, (j + 1) * LANES)
        (o2,) = _attend([q[:, sl]], [(k[:, sl], v[:, sl], None)], None)
        o_ref[:, sl] = o2.astype(o_ref.dtype)


def _ctx_attn_call(q, k, v, sink):
    b_n, l_n, nq = q.shape
    bmap = lambda b: (b, 0, 0)
    in_specs = [pl.BlockSpec((None, l_n, nq), bmap),
                pl.BlockSpec((None, l_n, k.shape[2]), bmap),
                pl.BlockSpec((None, l_n, v.shape[2]), bmap)]
    args = [q, k, v]
    kern = _ctx_attn_b_kernel
    if sink is not None:
        in_specs = [pl.BlockSpec(memory_space=pltpu.SMEM)] + in_specs
        args = [sink] + args
        kern = _ctx_attn_a_kernel
    return pl.pallas_call(
        kern,
        grid=(b_n,),
        in_specs=in_specs,
        out_specs=pl.BlockSpec((None, l_n, nq), bmap),
        out_shape=jax.ShapeDtypeStruct((b_n, l_n, nq), BF16),
        compiler_params=_cparams(1),
        name="ctx_attn",
    )(*args)


def _win_attn_kernel(sink_ref, q_ref, kp_ref, kc_ref, kn_ref, vp_ref, vc_ref, vn_ref,
                     ck_ref, cv_ref, o_ref):
    i = pl.program_id(1)
    nb = pl.num_programs(1)
    blk = q_ref.shape[0]
    q = q_ref[...]
    k_loc = jnp.concatenate([kp_ref[...], kc_ref[...], kn_ref[...]], axis=0)
    v_loc = jnp.concatenate([vp_ref[...], vc_ref[...], vn_ref[...]], axis=0)
    k_ctx = _dup_f32_to_bf16(ck_ref[...])
    v_ctx = _dup_f32_to_bf16(cv_ref[...])
    a = lax.broadcasted_iota(jnp.int32, (4 * blk, 3 * blk), 0) % blk
    j = lax.broadcasted_iota(jnp.int32, (4 * blk, 3 * blk), 1)
    valid = (j >= a) & (j <= a + 2 * WINDOW)
    valid &= (j >= blk) | (i > 0)
    valid &= (j < 2 * blk) | (i < nb - 1)
    bias = jnp.where(valid, 0.0, NEG).astype(F32)
    tile = lambda t, g: t[:, g * LANES:(g + 1) * LANES]
    _gqa_heads(q, lambda g: [(tile(k_loc, g), tile(v_loc, g), bias),
                             (tile(k_ctx, g), tile(v_ctx, g), None)], sink_ref, o_ref)


def _win_attn_call(q, kd, vd, ck, cv, sink):
    b_n, l_n, nq = q.shape
    blk = WINDOW
    nb = l_n // blk
    wkv = kd.shape[2]
    prev = lambda b, i: (b, jnp.maximum(i - 1, 0), 0)
    cur = lambda b, i: (b, i, 0)
    nxt = lambda b, i: (b, jnp.minimum(i + 1, nb - 1), 0)
    kv_specs = [pl.BlockSpec((None, blk, wkv), m) for m in (prev, cur, nxt)]
    ctx_spec = pl.BlockSpec((None,) + ck.shape[1:], lambda b, i: (b, 0, 0))
    return pl.pallas_call(
        _win_attn_kernel,
        grid=(b_n, nb),
        in_specs=[pl.BlockSpec(memory_space=pltpu.SMEM),
                  pl.BlockSpec((None, blk, nq), cur)] + kv_specs + kv_specs + [ctx_spec, ctx_spec],
        out_specs=pl.BlockSpec((None, blk, nq), cur),
        out_shape=jax.ShapeDtypeStruct((b_n, l_n, nq), BF16),
        compiler_params=_cparams(2),
        name="win_attn",
    )(sink, q, kd, kd, kd, vd, vd, vd, ck, cv)


def _row_start(r, rows):
    return jnp.clip(r - WIN_R // 2, 0, rows - WIN_R)


def _nbr_attn_kernel(q_ref, k_ref, v_ref, bias_ref, ck_ref, cv_ref, o_ref):
    q = q_ref[...]
    k = k_ref[0]
    v = v_ref[0]
    k_ctx = ck_ref[...].astype(BF16)
    v_ctx = cv_ref[...].astype(BF16)
    for j in range(N_HEADS // 2):
        sl = slice(j * LANES, (j + 1) * LANES)
        bias = jnp.concatenate([bias_ref[2 * j], bias_ref[2 * j + 1]], axis=0)
        (o2,) = _attend([q[:, sl]], [(k[:, sl], v[:, sl], bias),
                                     (k_ctx[:, sl], v_ctx[:, sl], None)], None)
        o_ref[:, sl] = o2.astype(o_ref.dtype)


def _nbr_attn_call(q, k, v, bias_tab, ck, cv):
    b_n, l_n, nq = q.shape
    rows = l_n // GRID_W
    nk = WIN_R * GRID_W
    row = lambda b, r: (b, r, 0)
    ctx_spec = pl.BlockSpec((None,) + ck.shape[1:], lambda b, r: (b, 0, 0))
    kv_spec = pl.BlockSpec((pl.Element(1), pl.Element(nk), pl.Element(nq)),
                           lambda b, r: (b, _row_start(r, rows) * GRID_W, 0))
    bias_spec = pl.BlockSpec((N_HEADS, None, GRID_W, nk),
                             lambda b, r: (0, _row_start(r, rows) - r + WIN_R - 1, 0, 0))
    return pl.pallas_call(
        _nbr_attn_kernel,
        grid=(b_n, rows),
        in_specs=[pl.BlockSpec((None, GRID_W, nq), row), kv_spec, kv_spec, bias_spec,
                  ctx_spec, ctx_spec],
        out_specs=pl.BlockSpec((None, GRID_W, nq), row),
        out_shape=jax.ShapeDtypeStruct((b_n, l_n, nq), BF16),
        compiler_params=_cparams(2),
        name="nbr_attn",
    )(q, k, v, bias_tab, ck, cv)


def _proj_kernel(x_ref, o_ref, mod_ref, w_ref, y_ref):
    gate = mod_ref[:, 2 * D_MODEL:3 * D_MODEL]
    y_ref[...] = x_ref[...] + gate * jnp.dot(o_ref[...], w_ref[...], preferred_element_type=F32)


def _proj_call(x, o, mod4, layer, row0, w, *, tm):
    g_n, l_n, _ = x.shape
    xmap = lambda g, i: (g, i, 0)
    return pl.pallas_call(
        _proj_kernel,
        grid=(g_n, l_n // tm),
        in_specs=[pl.BlockSpec((None, tm, D_MODEL), xmap),
                  pl.BlockSpec((None, tm, D_MODEL), xmap),
                  pl.BlockSpec((None, None, 1, 6 * D_MODEL), lambda g, i: (layer, row0 + g, 0, 0)),
                  pl.BlockSpec(w.shape, lambda g, i: (0, 0))],
        out_specs=pl.BlockSpec((None, tm, D_MODEL), xmap),
        out_shape=jax.ShapeDtypeStruct(x.shape, F32),
        compiler_params=_cparams(2),
        name="attn_proj",
    )(x, o, mod4, w)


def _ffn_kernel(x_ref, xp_ref, xn_ref, mod_ref, nw_ref, wup_ref, cw_ref, cb_ref, wdn_ref,
                y_ref, ug_ref, uv_ref, *, tiles_per_seq):
    tm = x_ref.shape[0]
    i = pl.program_id(1) % tiles_per_seq
    nw = nw_ref[...]
    shift = mod_ref[:, 3 * D_MODEL:4 * D_MODEL]
    scale = mod_ref[:, 4 * D_MODEL:5 * D_MODEL]
    gate = mod_ref[:, 5 * D_MODEL:6 * D_MODEL]
    x = x_ref[...]
    hp = jnp.where(i > 0, _rms_modulate(xp_ref[...], nw, shift, scale), 0.0)
    hn = jnp.where(i < tiles_per_seq - 1, _rms_modulate(xn_ref[...], nw, shift, scale), 0.0)
    h = jnp.concatenate([hp, _rms_modulate(x, nw, shift, scale), hn], axis=0).astype(BF16)

    def conv(u_ref, col):
        cw = cw_ref[:, col:col + FF_CHUNK]
        out = cb_ref[:, col:col + FF_CHUNK]
        for o in range(3):
            out = out + u_ref[pl.ds(HALO - 1 + o, tm), :] * cw[o:o + 1, :]
        return out

    acc = jnp.zeros((tm, D_MODEL), F32)
    for c in range(D_FF // FF_CHUNK):
        gc = c * FF_CHUNK
        vc = D_FF + c * FF_CHUNK
        ug_ref[...] = jnp.dot(h, wup_ref[:, gc:gc + FF_CHUNK], preferred_element_type=F32)
        uv_ref[...] = jnp.dot(h, wup_ref[:, vc:vc + FF_CHUNK], preferred_element_type=F32)
        act = (_silu(conv(ug_ref, gc)) * conv(uv_ref, vc)).astype(BF16)
        acc = acc + jnp.dot(act, wdn_ref[gc:gc + FF_CHUNK, :], preferred_element_type=F32)
    y_ref[...] = x + gate * acc


def _ffn_call(x, mod4, layer, row0, nw, w_up, conv_w, conv_b, w_down, *, seq_len, tm):
    g_n, l_n, _ = x.shape
    tiles_per_seq = seq_len // tm
    nh = l_n // HALO
    hb = tm // HALO
    const = lambda g, i: (0, 0)
    return pl.pallas_call(
        functools.partial(_ffn_kernel, tiles_per_seq=tiles_per_seq),
        grid=(g_n, l_n // tm),
        in_specs=[
            pl.BlockSpec((None, tm, D_MODEL), lambda g, i: (g, i, 0)),
            pl.BlockSpec((None, HALO, D_MODEL), lambda g, i: (g, jnp.maximum(i * hb - 1, 0), 0)),
            pl.BlockSpec((None, HALO, D_MODEL),
                         lambda g, i: (g, jnp.minimum((i + 1) * hb, nh - 1), 0)),
            pl.BlockSpec((None, None, 1, 6 * D_MODEL), lambda g, i: (layer, row0 + g, 0, 0)),
            pl.BlockSpec((1, D_MODEL), const),
            pl.BlockSpec(w_up.shape, const),
            pl.BlockSpec(conv_w.shape, const),
            pl.BlockSpec(conv_b.shape, const),
            pl.BlockSpec(w_down.shape, const),
        ],
        out_specs=pl.BlockSpec((None, tm, D_MODEL), lambda g, i: (g, i, 0)),
        out_shape=jax.ShapeDtypeStruct(x.shape, F32),
        scratch_shapes=[pltpu.VMEM((tm + 2 * HALO, FF_CHUNK), F32)] * 2,
        compiler_params=_cparams(2),
        name="conv_ffn",
    )(x, x, x, mod4, nw, w_up, conv_w, conv_b, w_down)


def _rope_tables(l_n):
    half = HEAD_DIM // 4
    freqs = ROPE_BASE ** (-jnp.arange(half, dtype=F32) / half)
    t = jnp.arange(l_n)
    lane = jnp.arange(HEAD_DIM)
    pos = jnp.where(lane[None, :] < HEAD_DIM // 2, (t // GRID_W)[:, None], (t % GRID_W)[:, None])
    ang = pos.astype(F32) * freqs[lane % half][None, :]
    sign = jnp.where((lane % (2 * half)) < half, -1.0, 1.0).astype(F32)
    cos = jnp.cos(ang)
    sin = jnp.sin(ang) * sign[None, :]
    return jnp.tile(cos, (1, 2)), jnp.tile(sin, (1, 2))


def _nbr_bias_table(rpb):
    col = jnp.arange(GRID_W)
    cs = jnp.clip(col - WIN_C // 2, 0, GRID_W - WIN_C)
    col_ok = (col[None, :] >= cs[:, None]) & (col[None, :] < cs[:, None] + WIN_C)
    dc = jnp.clip(col[None, :] - col[:, None], -(WIN_C - 1), WIN_C - 1) + WIN_C - 1
    dr = jnp.arange(WIN_R)[:, None] + jnp.arange(WIN_R)[None, :]
    b = rpb.astype(F32)[:, dr[:, None, :, None], dc[None, :, None, :]]
    b = jnp.where(col_ok[None, None, :, None, :], b, NEG)
    return b.reshape(rpb.shape[0], WIN_R, GRID_W, WIN_R * GRID_W)


def _block_diag_mean():
    r = jnp.arange(NORM_CHUNK) // HEAD_DIM
    return jnp.where(r[:, None] == r[None, :], 1.0 / HEAD_DIM, 0.0).astype(BF16)


def kernel(x_prompt, x_sample, cache_k_a, cache_v_a, cache_k_b, cache_v_b, c, c_ctx,
           norm_attn_w, norm_ffn_w, w_ada, b_ada,
           w_qkv_a, q_norm_a, k_norm_a, sink_a, w_o_a,
           w_qkv_b, q_norm_b, k_norm_b, rpb_b, w_o_b,
           w_up, conv_w, conv_b, w_down):
    depth = w_ada.shape[0]
    batch, seq, _ = x_prompt.shape
    dec_batch, dec_seq, _ = x_sample.shape
    nq = N_HEADS * HEAD_DIM

    cond = jnp.concatenate(
        [c_ctx[None, :], c, jnp.zeros((N_MOD_ROWS - 1 - dec_batch, D_MODEL), F32)], axis=0)
    mod = _ada_call(cond, w_ada, b_ada[:, None, :])
    mod4 = mod.reshape(depth, N_MOD_ROWS, 1, 6 * D_MODEL)

    bd = _block_diag_mean()
    rope_tabs = _rope_tables(dec_seq)
    tile_w = lambda w, n: jnp.tile(w, n)[None, :]

    xp = x_prompt.reshape(1, batch * seq, D_MODEL)
    xs = x_sample
    new_k_a, new_v_a, new_k_b, new_v_b = [], [], [], []
    for i in range(depth):
        j = i // 2
        nw_attn = norm_attn_w[i][None, :]
        nw_ffn = norm_ffn_w[i][None, :]
        if i % 2 == 0:
            nkv = N_KV_A * HEAD_DIM
            w_qkv = w_qkv_a[j].astype(BF16)
            w_o = w_o_a[j].astype(BF16)
            qn, kn = tile_w(q_norm_a[j], N_HEADS), tile_w(k_norm_a[j], N_KV_A)
            q, k, v, kd, vd = _qkv_call(xp, mod4, i, 0, nw_attn, w_qkv, qn, kn, bd, None,
                                        nkv=nkv, dup=True, f32_kv=True, tm=512)
            new_k_a.append(k.reshape(batch, seq, N_KV_A, HEAD_DIM))
            new_v_a.append(v.reshape(batch, seq, N_KV_A, HEAD_DIM))
            shp = (batch, seq, -1)
            op = _ctx_attn_call(q.reshape(shp), kd.reshape(shp), vd.reshape(shp), sink_a[j])
            q, kd, vd = _qkv_call(xs, mod4, i, 1, nw_attn, w_qkv, qn, kn, bd, rope_tabs,
                                  nkv=nkv, dup=True, f32_kv=False, tm=512)
            o_s = _win_attn_call(q, kd, vd,
                                 cache_k_a[:, j].reshape(dec_batch, -1, nkv),
                                 cache_v_a[:, j].reshape(dec_batch, -1, nkv), sink_a[j])
        else:
            nkv = nq
            w_qkv = w_qkv_b[j].astype(BF16)
            w_o = w_o_b[j].astype(BF16)
            qn, kn = tile_w(q_norm_b[j], N_HEADS), tile_w(k_norm_b[j], N_HEADS)
            q, k, v = _qkv_call(xp, mod4, i, 0, nw_attn, w_qkv, qn, kn, bd, None,
                                nkv=nkv, dup=False, f32_kv=True, tm=512)
            new_k_b.append(k.reshape(batch, seq, N_HEADS, HEAD_DIM))
            new_v_b.append(v.reshape(batch, seq, N_HEADS, HEAD_DIM))
            shp = (batch, seq, -1)
            op = _ctx_attn_call(q.reshape(shp), k.reshape(shp), v.reshape(shp), None)
            q, k, v = _qkv_call(xs, mod4, i, 1, nw_attn, w_qkv, qn, kn, bd, None,
                                nkv=nkv, dup=False, f32_kv=False, tm=512)
            o_s = _nbr_attn_call(q, k, v, _nbr_bias_table(rpb_b[j]),
                                 cache_k_b[:, j].reshape(dec_batch, -1, nq),
                                 cache_v_b[:, j].reshape(dec_batch, -1, nq))
        xp = _proj_call(xp, op.reshape(xp.shape), mod4, i, 0, w_o, tm=512)
        xs = _proj_call(xs, o_s, mod4, i, 1, w_o, tm=512)
        ffn_w = (nw_ffn, w_up[i].astype(BF16), conv_w[i], conv_b[i][None, :],
                 w_down[i].astype(BF16))
        xp = _ffn_call(xp, mod4, i, 0, *ffn_w, seq_len=seq, tm=seq)
        xs = _ffn_call(xs, mod4, i, 1, *ffn_w, seq_len=dec_seq, tm=512)
    return (xp.reshape(batch, seq, D_MODEL), xs,
            jnp.stack(new_k_a, axis=1), jnp.stack(new_v_a, axis=1),
            jnp.stack(new_k_b, axis=1), jnp.stack(new_v_b, axis=1))
```

```python
import functools

import jax
import jax.numpy as jnp
from jax import lax
from jax.experimental import pallas as pl
from jax.experimental.pallas import tpu as pltpu

D_MODEL = 1024
HEAD_DIM = 64
N_HEADS = 16
N_KV_A = 4
GRID_W = 64
WINDOW = 128
WIN_R = 8
WIN_C = 16
D_FF = 2816
ROPE_BASE = 10000.0
EPS = 1e-6
SCALE = HEAD_DIM ** -0.5

LANES = 128
HALO = 8
NORM_CHUNK = 256
FF_CHUNK = 256
FF_SLOTS = 3
NEG = -1e30
N_MOD_ROWS = 8

F32 = jnp.float32
BF16 = jnp.bfloat16
VMEM_LIMIT = 56 * 1024 * 1024


def _cparams(n_axes):
    return pltpu.CompilerParams(
        dimension_semantics=("arbitrary",) * n_axes, vmem_limit_bytes=VMEM_LIMIT)


def _silu(x):
    return x * (1.0 / (1.0 + jnp.exp(-x)))


def _rms_modulate(x, nw, shift, scale):
    ms = jnp.mean(x * x, axis=-1, keepdims=True)
    y = x * lax.rsqrt(ms + EPS) * nw
    return y * (1.0 + scale) + shift


def _ada_kernel(c_ref, w_ref, b_ref, o_ref):
    a = _silu(c_ref[...])
    o_ref[...] = jnp.dot(a, w_ref[...], precision=lax.Precision.HIGHEST,
                         preferred_element_type=F32) + b_ref[...]


def _ada_call(cond, w_ada, b_ada):
    depth = w_ada.shape[0]
    tn = 1024
    return pl.pallas_call(
        _ada_kernel,
        grid=(depth, 6 * D_MODEL // tn),
        in_specs=[
            pl.BlockSpec((N_MOD_ROWS, D_MODEL), lambda l, n: (0, 0)),
            pl.BlockSpec((None, D_MODEL, tn), lambda l, n: (l, 0, n)),
            pl.BlockSpec((None, 1, tn), lambda l, n: (l, 0, n)),
        ],
        out_specs=pl.BlockSpec((None, N_MOD_ROWS, tn), lambda l, n: (l, 0, n)),
        out_shape=jax.ShapeDtypeStruct((depth, N_MOD_ROWS, 6 * D_MODEL), F32),
        compiler_params=_cparams(2),
        name="adaln",
    )(cond, w_ada, b_ada)


def _head_rms(t, bd, w):
    outs = []
    for c in range(t.shape[1] // NORM_CHUNK):
        tc = t[:, c * NORM_CHUNK:(c + 1) * NORM_CHUNK]
        ms = jnp.dot((tc * tc).astype(BF16), bd, preferred_element_type=F32)
        outs.append(tc * lax.rsqrt(ms + EPS))
    return jnp.concatenate(outs, axis=1) * w


def _rope(t, cos, sin_signed, lo16):
    outs = []
    for c in range(t.shape[1] // LANES):
        tc = t[:, c * LANES:(c + 1) * LANES]
        partner = jnp.where(lo16, pltpu.roll(tc, LANES - 16, axis=1), pltpu.roll(tc, 16, axis=1))
        outs.append(tc * cos + partner * sin_signed)
    return jnp.concatenate(outs, axis=1)


def _dup_heads(t, lo64):
    outs = []
    for c in range(t.shape[1] // LANES):
        tc = t[:, c * LANES:(c + 1) * LANES]
        sw = pltpu.roll(tc, HEAD_DIM, axis=1)
        outs.append(jnp.where(lo64, tc, sw))
        outs.append(jnp.where(lo64, sw, tc))
    return jnp.concatenate(outs, axis=1)


def _qkv_kernel(*refs, nkv, rope, dup, f32_kv):
    x_ref, mod_ref, nw_ref, w_ref, qn_ref, kn_ref, bd_ref = refs[:7]
    pos = 7
    if rope:
        cos_ref, sin_ref = refs[pos:pos + 2]
        pos += 2
    outs = refs[pos:]
    nq = N_HEADS * HEAD_DIM
    lane = lax.broadcasted_iota(jnp.int32, (1, LANES), 1)

    h = _rms_modulate(x_ref[...], nw_ref[...], mod_ref[:, 0:D_MODEL],
                      mod_ref[:, D_MODEL:2 * D_MODEL])
    qkv = jnp.dot(h.astype(BF16), w_ref[...], preferred_element_type=F32)
    bd = bd_ref[...]
    q = _head_rms(qkv[:, :nq], bd, qn_ref[...])
    k = _head_rms(qkv[:, nq:nq + nkv], bd, kn_ref[...])
    v = qkv[:, nq + nkv:]
    if rope:
        lo16 = (lane & 16) == 0
        q = _rope(q, cos_ref[...], sin_ref[...], lo16)
        k = _rope(k, cos_ref[...], sin_ref[...], lo16)
    outs[0][...] = (q * SCALE).astype(BF16)
    o = 1
    if f32_kv:
        outs[o][...] = k
        outs[o + 1][...] = v
        o += 2
    if dup:
        lo64 = lane < HEAD_DIM
        k = _dup_heads(k, lo64)
        v = _dup_heads(v, lo64)
    if dup or not f32_kv:
        outs[o][...] = k.astype(BF16)
        outs[o + 1][...] = v.astype(BF16)


def _qkv_call(x, mod4, layer, row0, nw, w, qn, kn, bd, rope_tabs, *, nkv, dup, f32_kv, tm):
    g_n, l_n, _ = x.shape
    rope = rope_tabs is not None
    nq = N_HEADS * HEAD_DIM
    xmap = lambda g, i: (g, i, 0)
    const = lambda g, i: (0, 0)
    in_specs = [
        pl.BlockSpec((None, tm, D_MODEL), xmap),
        pl.BlockSpec((None, None, 1, 6 * D_MODEL), lambda g, i: (layer, row0 + g, 0, 0)),
        pl.BlockSpec((1, D_MODEL), const),
        pl.BlockSpec(w.shape, const),
        pl.BlockSpec((1, nq), const),
        pl.BlockSpec((1, nkv), const),
        pl.BlockSpec((NORM_CHUNK, NORM_CHUNK), const),
    ]
    args = [x, mod4, nw, w, qn, kn, bd]
    if rope:
        in_specs += [pl.BlockSpec((tm, LANES), lambda g, i: (i, 0))] * 2
        args += list(rope_tabs)
    out_shape = [jax.ShapeDtypeStruct((g_n, l_n, nq), BF16)]
    out_specs = [pl.BlockSpec((None, tm, nq), xmap)]
    if f32_kv:
        out_shape += [jax.ShapeDtypeStruct((g_n, l_n, nkv), F32)] * 2
        out_specs += [pl.BlockSpec((None, tm, nkv), xmap)] * 2
    if dup or not f32_kv:
        wkv = nkv * 2 if dup else nkv
        out_shape += [jax.ShapeDtypeStruct((g_n, l_n, wkv), BF16)] * 2
        out_specs += [pl.BlockSpec((None, tm, wkv), xmap)] * 2
    return pl.pallas_call(
        functools.partial(_qkv_kernel, nkv=nkv, rope=rope, dup=dup, f32_kv=f32_kv),
        grid=(g_n, l_n // tm),
        in_specs=in_specs,
        out_specs=out_specs,
        out_shape=out_shape,
        compiler_params=_cparams(2),
        name="qkv",
    )(*args)


def _attend(q2_list, segs, sinks):
    m_rows = q2_list[0].shape[0]
    lo = lax.broadcasted_iota(jnp.int32, (1, LANES), 1) < HEAD_DIM
    zero = jnp.zeros((), BF16)
    stacked = []
    for q2 in q2_list:
        stacked.append(jnp.where(lo, q2, zero))
        stacked.append(jnp.where(lo, zero, q2))
    qs = jnp.concatenate(stacked, axis=0)
    scores = []
    for k2, _, bias in segs:
        s = lax.dot_general(qs, k2, (((1,), (1,)), ((), ())), preferred_element_type=F32)
        scores.append(s if bias is None else s + bias)
    mx = functools.reduce(jnp.maximum, [jnp.max(s, axis=-1, keepdims=True) for s in scores])
    if sinks is not None:
        sink_col = jnp.concatenate(
            [jnp.full((m_rows, 1), sk, F32) for sk in sinks], axis=0)
        mx = jnp.maximum(mx, sink_col)
    den = jnp.exp(sink_col - mx) if sinks is not None else jnp.zeros_like(mx)
    acc = jnp.zeros((qs.shape[0], LANES), F32)
    for s, (_, v2, _) in zip(scores, segs):
        p = jnp.exp(s - mx)
        den = den + jnp.sum(p, axis=-1, keepdims=True)
        acc = acc + jnp.dot(p.astype(BF16), v2, preferred_element_type=F32)
    acc = acc / den
    return [jnp.where(lo, acc[(2 * i) * m_rows:(2 * i + 1) * m_rows],
                      acc[(2 * i + 1) * m_rows:(2 * i + 2) * m_rows])
            for i in range(len(q2_list))]


def _dup_f32_to_bf16(t):
    lo64 = lax.broadcasted_iota(jnp.int32, (1, LANES), 1) < HEAD_DIM
    return _dup_heads(t, lo64).astype(BF16)


def _gqa_heads(q, segs_of_group, sink_ref, o_ref):
    for g in range(N_KV_A):
        q2s = [q[:, (2 * g + i) * LANES:(2 * g + i + 1) * LANES] for i in range(2)]
        sinks = [sink_ref[4 * g + i] for i in range(4)]
        outs = _attend(q2s, segs_of_group(g), sinks)
        for i in range(2):
            o_ref[:, (2 * g + i) * LANES:(2 * g + i + 1) * LANES] = outs[i].astype(o_ref.dtype)


def _ctx_attn_a_kernel(sink_ref, q_ref, k_ref, v_ref, o_ref):
    q = q_ref[...]
    k = k_ref[...]
    v = v_ref[...]
    tile = lambda t, g: t[:, g * LANES:(g + 1) * LANES]
    _gqa_heads(q, lambda g: [(tile(k, g), tile(v, g), None)], sink_ref, o_ref)


def _ctx_attn_b_kernel(q_ref, k_ref, v_ref, o_ref):
    q = q_ref[...]
    k = k_ref[...].astype(BF16)
    v = v_ref[...].astype(BF16)
    for j in range(N_HEADS // 2):
        sl = slice(j * LANES, (j + 1) * LANES)
        (o2,) = _attend([q[:, sl]], [(k[:, sl], v[:, sl], None)], None)
        o_ref[:, sl] = o2.astype(o_ref.dtype)


def _ctx_attn_call(q, k, v, sink):
    b_n, l_n, nq = q.shape
    bmap = lambda b: (b, 0, 0)
    in_specs = [pl.BlockSpec((None, l_n, nq), bmap),
                pl.BlockSpec((None, l_n, k.shape[2]), bmap),
                pl.BlockSpec((None, l_n, v.shape[2]), bmap)]
    args = [q, k, v]
    kern = _ctx_attn_b_kernel
    if sink is not None:
        in_specs = [pl.BlockSpec(memory_space=pltpu.SMEM)] + in_specs
        args = [sink] + args
        kern = _ctx_attn_a_kernel
    return pl.pallas_call(
        kern,
        grid=(b_n,),
        in_specs=in_specs,
        out_specs=pl.BlockSpec((None, l_n, nq), bmap),
        out_shape=jax.ShapeDtypeStruct((b_n, l_n, nq), BF16),
        compiler_params=_cparams(1),
        name="ctx_attn",
    )(*args)


def _win_attn_kernel(sink_ref, q_ref, kp_ref, kc_ref, kn_ref, vp_ref, vc_ref, vn_ref,
                     ck_ref, cv_ref, o_ref):
    i = pl.program_id(1)
    nb = pl.num_programs(1)
    blk = q_ref.shape[0]
    q = q_ref[...]
    k_loc = jnp.concatenate([kp_ref[...], kc_ref[...], kn_ref[...]], axis=0)
    v_loc = jnp.concatenate([vp_ref[...], vc_ref[...], vn_ref[...]], axis=0)
    k_ctx = _dup_f32_to_bf16(ck_ref[...])
    v_ctx = _dup_f32_to_bf16(cv_ref[...])
    a = lax.broadcasted_iota(jnp.int32, (4 * blk, 3 * blk), 0) % blk
    j = lax.broadcasted_iota(jnp.int32, (4 * blk, 3 * blk), 1)
    valid = (j >= a) & (j <= a + 2 * WINDOW)
    valid &= (j >= blk) | (i > 0)
    valid &= (j < 2 * blk) | (i < nb - 1)
    bias = jnp.where(valid, 0.0, NEG).astype(F32)
    tile = lambda t, g: t[:, g * LANES:(g + 1) * LANES]
    _gqa_heads(q, lambda g: [(tile(k_loc, g), tile(v_loc, g), bias),
                             (tile(k_ctx, g), tile(v_ctx, g), None)], sink_ref, o_ref)


def _win_attn_call(q, kd, vd, ck, cv, sink):
    b_n, l_n, nq = q.shape
    blk = WINDOW
    nb = l_n // blk
    wkv = kd.shape[2]
    prev = lambda b, i: (b, jnp.maximum(i - 1, 0), 0)
    cur = lambda b, i: (b, i, 0)
    nxt = lambda b, i: (b, jnp.minimum(i + 1, nb - 1), 0)
    kv_specs = [pl.BlockSpec((None, blk, wkv), m) for m in (prev, cur, nxt)]
    ctx_spec = pl.BlockSpec((None,) + ck.shape[1:], lambda b, i: (b, 0, 0))
    return pl.pallas_call(
        _win_attn_kernel,
        grid=(b_n, nb),
        in_specs=[pl.BlockSpec(memory_space=pltpu.SMEM),
                  pl.BlockSpec((None, blk, nq), cur)] + kv_specs + kv_specs + [ctx_spec, ctx_spec],
        out_specs=pl.BlockSpec((None, blk, nq), cur),
        out_shape=jax.ShapeDtypeStruct((b_n, l_n, nq), BF16),
        compiler_params=_cparams(2),
        name="win_attn",
    )(sink, q, kd, kd, kd, vd, vd, vd, ck, cv)


def _row_start(r, rows):
    return jnp.clip(r - WIN_R // 2, 0, rows - WIN_R)


def _nbr_attn_kernel(q_ref, k_ref, v_ref, bias_ref, ck_ref, cv_ref, o_ref):
    q = q_ref[...]
    k = k_ref[0]
    v = v_ref[0]
    k_ctx = ck_ref[...].astype(BF16)
    v_ctx = cv_ref[...].astype(BF16)
    for j in range(N_HEADS // 2):
        sl = slice(j * LANES, (j + 1) * LANES)
        bias = jnp.concatenate([bias_ref[2 * j], bias_ref[2 * j + 1]], axis=0)
        (o2,) = _attend([q[:, sl]], [(k[:, sl], v[:, sl], bias),
                                     (k_ctx[:, sl], v_ctx[:, sl], None)], None)
        o_ref[:, sl] = o2.astype(o_ref.dtype)


def _nbr_attn_call(q, k, v, bias_tab, ck, cv):
    b_n, l_n, nq = q.shape
    rows = l_n // GRID_W
    nk = WIN_R * GRID_W
    row = lambda b, r: (b, r, 0)
    ctx_spec = pl.BlockSpec((None,) + ck.shape[1:], lambda b, r: (b, 0, 0))
    kv_spec = pl.BlockSpec((pl.Element(1), pl.Element(nk), pl.Element(nq)),
                           lambda b, r: (b, _row_start(r, rows) * GRID_W, 0))
    bias_spec = pl.BlockSpec((N_HEADS, None, GRID_W, nk),
                             lambda b, r: (0, _row_start(r, rows) - r + WIN_R - 1, 0, 0))
    return pl.pallas_call(
        _nbr_attn_kernel,
        grid=(b_n, rows),
        in_specs=[pl.BlockSpec((None, GRID_W, nq), row), kv_spec, kv_spec, bias_spec,
                  ctx_spec, ctx_spec],
        out_specs=pl.BlockSpec((None, GRID_W, nq), row),
        out_shape=jax.ShapeDtypeStruct((b_n, l_n, nq), BF16),
        compiler_params=_cparams(2),
        name="nbr_attn",
    )(q, k, v, bias_tab, ck, cv)


def _proj_kernel(x_ref, o_ref, mod_ref, w_ref, y_ref):
    gate = mod_ref[:, 2 * D_MODEL:3 * D_MODEL]
    y_ref[...] = x_ref[...] + gate * jnp.dot(o_ref[...], w_ref[...], preferred_element_type=F32)


def _proj_call(x, o, mod4, layer, row0, w, *, tm):
    g_n, l_n, _ = x.shape
    xmap = lambda g, i: (g, i, 0)
    return pl.pallas_call(
        _proj_kernel,
        grid=(g_n, l_n // tm),
        in_specs=[pl.BlockSpec((None, tm, D_MODEL), xmap),
                  pl.BlockSpec((None, tm, D_MODEL), xmap),
                  pl.BlockSpec((None, None, 1, 6 * D_MODEL), lambda g, i: (layer, row0 + g, 0, 0)),
                  pl.BlockSpec(w.shape, lambda g, i: (0, 0))],
        out_specs=pl.BlockSpec((None, tm, D_MODEL), xmap),
        out_shape=jax.ShapeDtypeStruct(x.shape, F32),
        compiler_params=_cparams(2),
        name="attn_proj",
    )(x, o, mod4, w)


def _ffn_kernel(x_ref, xp_ref, xn_ref, mod_ref, nw_ref, wup_ref, cw_ref, cb_ref, wdn_ref,
                y_ref, *u_refs, tiles_per_seq):
    tm = x_ref.shape[0]
    i = pl.program_id(1) % tiles_per_seq
    nw = nw_ref[...]
    shift = mod_ref[:, 3 * D_MODEL:4 * D_MODEL]
    scale = mod_ref[:, 4 * D_MODEL:5 * D_MODEL]
    gate = mod_ref[:, 5 * D_MODEL:6 * D_MODEL]
    x = x_ref[...]
    hp = jnp.where(i > 0, _rms_modulate(xp_ref[...], nw, shift, scale), 0.0)
    hn = jnp.where(i < tiles_per_seq - 1, _rms_modulate(xn_ref[...], nw, shift, scale), 0.0)
    h = jnp.concatenate([hp, _rms_modulate(x, nw, shift, scale), hn], axis=0).astype(BF16)

    n_slabs = FF_CHUNK // LANES

    def conv(u_ref, col):
        outs = []
        for s in range(n_slabs):
            lanes = slice(col + s * LANES, col + (s + 1) * LANES)
            cw = cw_ref[:, lanes]
            out = cb_ref[:, lanes]
            for o in range(3):
                out = out + u_ref[s, pl.ds(HALO - 1 + o, tm), :] * cw[o:o + 1, :]
            outs.append(out)
        return jnp.concatenate(outs, axis=1)

    def up_one(u_ref, col):
        u = jnp.dot(h, wup_ref[:, col:col + FF_CHUNK], preferred_element_type=F32)
        for s in range(n_slabs):
            u_ref[s] = u[:, s * LANES:(s + 1) * LANES]

    def up(c):
        up_one(u_refs[2 * (c % FF_SLOTS)], c * FF_CHUNK)
        up_one(u_refs[2 * (c % FF_SLOTS) + 1], D_FF + c * FF_CHUNK)

    n_chunks = D_FF // FF_CHUNK
    acc = jnp.zeros((tm, D_MODEL), F32)
    up(0)
    for c in range(n_chunks):
        if c + 1 < n_chunks:
            up(c + 1)
        ug_ref, uv_ref = u_refs[2 * (c % FF_SLOTS)], u_refs[2 * (c % FF_SLOTS) + 1]
        gc = c * FF_CHUNK
        act = (_silu(conv(ug_ref, gc)) * conv(uv_ref, D_FF + gc)).astype(BF16)
        acc = acc + jnp.dot(act, wdn_ref[gc:gc + FF_CHUNK, :], preferred_element_type=F32)
    y_ref[...] = x + gate * acc


def _ffn_call(x, mod4, layer, row0, nw, w_up, conv_w, conv_b, w_down, *, seq_len, tm):
    g_n, l_n, _ = x.shape
    tiles_per_seq = seq_len // tm
    nh = l_n // HALO
    hb = tm // HALO
    const = lambda g, i: (0, 0)
    return pl.pallas_call(
        functools.partial(_ffn_kernel, tiles_per_seq=tiles_per_seq),
        grid=(g_n, l_n // tm),
        in_specs=[
            pl.BlockSpec((None, tm, D_MODEL), lambda g, i: (g, i, 0)),
            pl.BlockSpec((None, HALO, D_MODEL), lambda g, i: (g, jnp.maximum(i * hb - 1, 0), 0)),
            pl.BlockSpec((None, HALO, D_MODEL),
                         lambda g, i: (g, jnp.minimum((i + 1) * hb, nh - 1), 0)),
            pl.BlockSpec((None, None, 1, 6 * D_MODEL), lambda g, i: (layer, row0 + g, 0, 0)),
            pl.BlockSpec((1, D_MODEL), const),
            pl.BlockSpec(w_up.shape, const),
            pl.BlockSpec(conv_w.shape, const),
            pl.BlockSpec(conv_b.shape, const),
            pl.BlockSpec(w_down.shape, const),
        ],
        out_specs=pl.BlockSpec((None, tm, D_MODEL), lambda g, i: (g, i, 0)),
        out_shape=jax.ShapeDtypeStruct(x.shape, F32),
        scratch_shapes=[pltpu.VMEM((FF_CHUNK // LANES, tm + 2 * HALO, LANES), F32)]
        * (2 * FF_SLOTS),
        compiler_params=_cparams(2),
        name="conv_ffn",
    )(x, x, x, mod4, nw, w_up, conv_w, conv_b, w_down)


def _rope_tables(l_n):
    half = HEAD_DIM // 4
    freqs = ROPE_BASE ** (-jnp.arange(half, dtype=F32) / half)
    t = jnp.arange(l_n)
    lane = jnp.arange(HEAD_DIM)
    pos = jnp.where(lane[None, :] < HEAD_DIM // 2, (t // GRID_W)[:, None], (t % GRID_W)[:, None])
    ang = pos.astype(F32) * freqs[lane % half][None, :]
    sign = jnp.where((lane % (2 * half)) < half, -1.0, 1.0).astype(F32)
    cos = jnp.cos(ang)
    sin = jnp.sin(ang) * sign[None, :]
    return jnp.tile(cos, (1, 2)), jnp.tile(sin, (1, 2))


def _nbr_bias_kernel(rpb_ref, o_ref):
    n_dr = 2 * WIN_R - 1
    n_dc = 2 * WIN_C - 1
    base = pl.program_id(0) * (n_dr * n_dc)
    qc = lax.broadcasted_iota(jnp.int32, (GRID_W, LANES), 0)
    lane = lax.broadcasted_iota(jnp.int32, (GRID_W, LANES), 1)
    kc = lane % GRID_W
    dc = jnp.clip(kc - qc, -(WIN_C - 1), WIN_C - 1) + WIN_C - 1
    cs = jnp.clip(qc - WIN_C // 2, 0, GRID_W - WIN_C)
    col_ok = (kc >= cs) & (kc < cs + WIN_C)
    blocks = []
    for a in range(n_dr):
        t = jnp.full((GRID_W, LANES), NEG, F32)
        for d in range(n_dc):
            t = jnp.where(col_ok & (dc == d), rpb_ref[base + a * n_dc + d], t)
        blocks.append(t)
    lo = lane < GRID_W
    for cls in range(WIN_R):
        for m in range(WIN_R // 2):
            o_ref[cls, :, m * LANES:(m + 1) * LANES] = jnp.where(
                lo, blocks[cls + 2 * m], blocks[cls + 2 * m + 1])


def _nbr_bias_table(rpb):
    n_heads = rpb.shape[0]
    return pl.pallas_call(
        _nbr_bias_kernel,
        grid=(n_heads,),
        in_specs=[pl.BlockSpec(memory_space=pltpu.SMEM)],
        out_specs=pl.BlockSpec((None, WIN_R, GRID_W, WIN_R * GRID_W), lambda h: (h, 0, 0, 0)),
        out_shape=jax.ShapeDtypeStruct((n_heads, WIN_R, GRID_W, WIN_R * GRID_W), F32),
        compiler_params=_cparams(1),
        name="nbr_bias",
    )(rpb.astype(F32).reshape(-1))


def _block_diag_mean():
    r = jnp.arange(NORM_CHUNK) // HEAD_DIM
    return jnp.where(r[:, None] == r[None, :], 1.0 / HEAD_DIM, 0.0).astype(BF16)


def kernel(x_prompt, x_sample, cache_k_a, cache_v_a, cache_k_b, cache_v_b, c, c_ctx,
           norm_attn_w, norm_ffn_w, w_ada, b_ada,
           w_qkv_a, q_norm_a, k_norm_a, sink_a, w_o_a,
           w_qkv_b, q_norm_b, k_norm_b, rpb_b, w_o_b,
           w_up, conv_w, conv_b, w_down):
    depth = w_ada.shape[0]
    batch, seq, _ = x_prompt.shape
    dec_batch, dec_seq, _ = x_sample.shape
    nq = N_HEADS * HEAD_DIM

    cond = jnp.concatenate(
        [c_ctx[None, :], c, jnp.zeros((N_MOD_ROWS - 1 - dec_batch, D_MODEL), F32)], axis=0)
    mod = _ada_call(cond, w_ada, b_ada[:, None, :])
    mod4 = mod.reshape(depth, N_MOD_ROWS, 1, 6 * D_MODEL)

    bd = _block_diag_mean()
    rope_tabs = _rope_tables(dec_seq)
    tile_w = lambda w, n: jnp.tile(w, n)[None, :]

    xp = x_prompt.reshape(1, batch * seq, D_MODEL)
    xs = x_sample
    new_k_a, new_v_a, new_k_b, new_v_b = [], [], [], []
    for i in range(depth):
        j = i // 2
        nw_attn = norm_attn_w[i][None, :]
        nw_ffn = norm_ffn_w[i][None, :]
        if i % 2 == 0:
            nkv = N_KV_A * HEAD_DIM
            w_qkv = w_qkv_a[j].astype(BF16)
            w_o = w_o_a[j].astype(BF16)
            qn, kn = tile_w(q_norm_a[j], N_HEADS), tile_w(k_norm_a[j], N_KV_A)
            q, k, v, kd, vd = _qkv_call(xp, mod4, i, 0, nw_attn, w_qkv, qn, kn, bd, None,
                                        nkv=nkv, dup=True, f32_kv=True, tm=512)
            new_k_a.append(k.reshape(batch, seq, N_KV_A, HEAD_DIM))
            new_v_a.append(v.reshape(batch, seq, N_KV_A, HEAD_DIM))
            shp = (batch, seq, -1)
            op = _ctx_attn_call(q.reshape(shp), kd.reshape(shp), vd.reshape(shp), sink_a[j])
            q, kd, vd = _qkv_call(xs, mod4, i, 1, nw_attn, w_qkv, qn, kn, bd, rope_tabs,
                                  nkv=nkv, dup=True, f32_kv=False, tm=512)
            o_s = _win_attn_call(q, kd, vd,
                                 cache_k_a[:, j].reshape(dec_batch, -1, nkv),
                                 cache_v_a[:, j].reshape(dec_batch, -1, nkv), sink_a[j])
        else:
            nkv = nq
            w_qkv = w_qkv_b[j].astype(BF16)
            w_o = w_o_b[j].astype(BF16)
            qn, kn = tile_w(q_norm_b[j], N_HEADS), tile_w(k_norm_b[j], N_HEADS)
            q, k, v = _qkv_call(xp, mod4, i, 0, nw_attn, w_qkv, qn, kn, bd, None,
                                nkv=nkv, dup=False, f32_kv=True, tm=512)
            new_k_b.append(k.reshape(batch, seq, N_HEADS, HEAD_DIM))
            new_v_b.append(v.reshape(batch, seq, N_HEADS, HEAD_DIM))
            shp = (batch, seq, -1)
            op = _ctx_attn_call(q.reshape(shp), k.reshape(shp), v.reshape(shp), None)
            q, k, v = _qkv_call(xs, mod4, i, 1, nw_attn, w_qkv, qn, kn, bd, None,
                                nkv=nkv, dup=False, f32_kv=False, tm=512)
            o_s = _nbr_attn_call(q, k, v, _nbr_bias_table(rpb_b[j]),
                                 cache_k_b[:, j].reshape(dec_batch, -1, nq),
                                 cache_v_b[:, j].reshape(dec_batch, -1, nq))
        xp = _proj_call(xp, op.reshape(xp.shape), mod4, i, 0, w_o, tm=512)
        xs = _proj_call(xs, o_s, mod4, i, 1, w_o, tm=512)
        ffn_w = (nw_ffn, w_up[i].astype(BF16), conv_w[i], conv_b[i][None, :],
                 w_down[i].astype(BF16))
        xp = _ffn_call(xp, mod4, i, 0, *ffn_w, seq_len=seq, tm=seq)
        xs = _ffn_call(xs, mod4, i, 1, *ffn_w, seq_len=dec_seq, tm=512)
    return (xp.reshape(batch, seq, D_MODEL), xs,
            jnp.stack(new_k_a, axis=1), jnp.stack(new_v_a, axis=1),
            jnp.stack(new_k_b, axis=1), jnp.stack(new_v_b, axis=1))
```

```python
import functools

import jax
import jax.numpy as jnp
from jax import lax
from jax.experimental import pallas as pl
from jax.experimental.pallas import tpu as pltpu

D_MODEL = 1024
HEAD_DIM = 64
N_HEADS = 16
N_KV_A = 4
GRID_W = 64
WINDOW = 128
WIN_R = 8
WIN_C = 16
D_FF = 2816
ROPE_BASE = 10000.0
EPS = 1e-6
SCALE = HEAD_DIM ** -0.5

LANES = 128
HALO = 16
NBR_ROWS = 4
NBR_WIN = NBR_ROWS + WIN_R
NORM_CHUNK = 256
FF_CHUNK = 256
FF_SLOTS = 3
NEG = -1e30
N_MOD_ROWS = 8

RESIDENT = pl.Buffered(1)

F32 = jnp.float32
BF16 = jnp.bfloat16
VMEM_LIMIT = 56 * 1024 * 1024


def _cparams(n_axes):
    return pltpu.CompilerParams(
        dimension_semantics=("arbitrary",) * n_axes, vmem_limit_bytes=VMEM_LIMIT)


def _silu(x):
    return x * (1.0 / (1.0 + jnp.exp(-x)))


def _rms_modulate(x, nw, shift, scale):
    ms = jnp.mean(x * x, axis=-1, keepdims=True)
    y = x * lax.rsqrt(ms + EPS) * nw
    return y * (1.0 + scale) + shift


def _ada_kernel(c_ref, w_ref, b_ref, o_ref):
    a = _silu(c_ref[...])
    o_ref[...] = jnp.dot(a, w_ref[...], precision=lax.Precision.HIGHEST,
                         preferred_element_type=F32) + b_ref[...]


def _ada_call(cond, w_ada, b_ada):
    depth = w_ada.shape[0]
    tn = 1024
    return pl.pallas_call(
        _ada_kernel,
        grid=(depth, 6 * D_MODEL // tn),
        in_specs=[
            pl.BlockSpec((N_MOD_ROWS, D_MODEL), lambda l, n: (0, 0)),
            pl.BlockSpec((None, D_MODEL, tn), lambda l, n: (l, 0, n)),
            pl.BlockSpec((None, 1, tn), lambda l, n: (l, 0, n)),
        ],
        out_specs=pl.BlockSpec((None, N_MOD_ROWS, tn), lambda l, n: (l, 0, n)),
        out_shape=jax.ShapeDtypeStruct((depth, N_MOD_ROWS, 6 * D_MODEL), F32),
        compiler_params=_cparams(2),
        name="adaln",
    )(cond, w_ada, b_ada)


def _head_rms(t, bd, w):
    outs = []
    for c in range(t.shape[1] // NORM_CHUNK):
        tc = t[:, c * NORM_CHUNK:(c + 1) * NORM_CHUNK]
        ms = jnp.dot((tc * tc).astype(BF16), bd, preferred_element_type=F32)
        outs.append(tc * lax.rsqrt(ms + EPS))
    return jnp.concatenate(outs, axis=1) * w


def _rope(t, cos, sin_signed, lo16):
    outs = []
    for c in range(t.shape[1] // LANES):
        tc = t[:, c * LANES:(c + 1) * LANES]
        partner = jnp.where(lo16, pltpu.roll(tc, LANES - 16, axis=1), pltpu.roll(tc, 16, axis=1))
        outs.append(tc * cos + partner * sin_signed)
    return jnp.concatenate(outs, axis=1)


def _dup_heads(t, lo64):
    outs = []
    for c in range(t.shape[1] // LANES):
        tc = t[:, c * LANES:(c + 1) * LANES]
        sw = pltpu.roll(tc, HEAD_DIM, axis=1)
        outs.append(jnp.where(lo64, tc, sw))
        outs.append(jnp.where(lo64, sw, tc))
    return jnp.concatenate(outs, axis=1)


def _qkv_kernel(*refs, nkv, rope, dup, f32_kv):
    x_ref, mod_ref, nw_ref, w_ref, qn_ref, kn_ref, bd_ref = refs[:7]
    pos = 7
    if rope:
        cos_ref, sin_ref = refs[pos:pos + 2]
        pos += 2
    outs = refs[pos:]
    nq = N_HEADS * HEAD_DIM
    lane = lax.broadcasted_iota(jnp.int32, (1, LANES), 1)

    h = _rms_modulate(x_ref[...], nw_ref[...], mod_ref[:, 0:D_MODEL],
                      mod_ref[:, D_MODEL:2 * D_MODEL])
    qkv = jnp.dot(h.astype(BF16), w_ref[...], preferred_element_type=F32)
    bd = bd_ref[...]
    q = _head_rms(qkv[:, :nq], bd, qn_ref[...])
    k = _head_rms(qkv[:, nq:nq + nkv], bd, kn_ref[...])
    v = qkv[:, nq + nkv:]
    if rope:
        lo16 = (lane & 16) == 0
        q = _rope(q, cos_ref[...], sin_ref[...], lo16)
        k = _rope(k, cos_ref[...], sin_ref[...], lo16)
    outs[0][...] = (q * SCALE).astype(BF16)
    o = 1
    if f32_kv:
        outs[o][...] = k
        outs[o + 1][...] = v
        o += 2
    if dup:
        lo64 = lane < HEAD_DIM
        k = _dup_heads(k, lo64)
        v = _dup_heads(v, lo64)
    if dup or not f32_kv:
        outs[o][...] = k.astype(BF16)
        outs[o + 1][...] = v.astype(BF16)


def _qkv_call(x, mod4, layer, row0, nw, w, qn, kn, bd, rope_tabs, *, nkv, dup, f32_kv, tm):
    g_n, l_n, _ = x.shape
    rope = rope_tabs is not None
    nq = N_HEADS * HEAD_DIM
    xmap = lambda g, i: (g, i, 0)
    const = lambda g, i: (0, 0)
    in_specs = [
        pl.BlockSpec((None, tm, D_MODEL), xmap),
        pl.BlockSpec((None, None, 1, 6 * D_MODEL), lambda g, i: (layer, row0 + g, 0, 0)),
        pl.BlockSpec((1, D_MODEL), const),
        pl.BlockSpec(w.shape, const),
        pl.BlockSpec((1, nq), const),
        pl.BlockSpec((1, nkv), const),
        pl.BlockSpec((NORM_CHUNK, NORM_CHUNK), const),
    ]
    args = [x, mod4, nw, w, qn, kn, bd]
    if rope:
        in_specs += [pl.BlockSpec((tm, LANES), lambda g, i: (i, 0))] * 2
        args += list(rope_tabs)
    out_shape = [jax.ShapeDtypeStruct((g_n, l_n, nq), BF16)]
    out_specs = [pl.BlockSpec((None, tm, nq), xmap)]
    if f32_kv:
        out_shape += [jax.ShapeDtypeStruct((g_n, l_n, nkv), F32)] * 2
        out_specs += [pl.BlockSpec((None, tm, nkv), xmap)] * 2
    if dup or not f32_kv:
        wkv = nkv * 2 if dup else nkv
        out_shape += [jax.ShapeDtypeStruct((g_n, l_n, wkv), BF16)] * 2
        out_specs += [pl.BlockSpec((None, tm, wkv), xmap)] * 2
    return pl.pallas_call(
        functools.partial(_qkv_kernel, nkv=nkv, rope=rope, dup=dup, f32_kv=f32_kv),
        grid=(g_n, l_n // tm),
        in_specs=in_specs,
        out_specs=out_specs,
        out_shape=out_shape,
        compiler_params=_cparams(2),
        name="qkv",
    )(*args)


def _attend(q2_list, segs, sinks):
    m_rows = q2_list[0].shape[0]
    lo = lax.broadcasted_iota(jnp.int32, (1, LANES), 1) < HEAD_DIM
    zero = jnp.zeros((), BF16)
    stacked = []
    for q2 in q2_list:
        stacked.append(jnp.where(lo, q2, zero))
        stacked.append(jnp.where(lo, zero, q2))
    qs = jnp.concatenate(stacked, axis=0)
    scores = []
    for k2, _, bias in segs:
        s = lax.dot_general(qs, k2, (((1,), (1,)), ((), ())), preferred_element_type=F32)
        scores.append(s if bias is None else s + bias)
    mx = functools.reduce(jnp.maximum, [jnp.max(s, axis=-1, keepdims=True) for s in scores])
    if sinks is not None:
        sink_col = jnp.concatenate(
            [jnp.full((m_rows, 1), sk, F32) for sk in sinks], axis=0)
        mx = jnp.maximum(mx, sink_col)
    den = jnp.exp(sink_col - mx) if sinks is not None else jnp.zeros_like(mx)
    acc = jnp.zeros((qs.shape[0], LANES), F32)
    for s, (_, v2, _) in zip(scores, segs):
        p = jnp.exp(s - mx)
        den = den + jnp.sum(p, axis=-1, keepdims=True)
        acc = acc + jnp.dot(p.astype(BF16), v2, preferred_element_type=F32)
    rows = lambda t, n: t[n * m_rows:(n + 1) * m_rows]
    out = acc / den
    return [jnp.where(lo, rows(out, 2 * i), rows(out, 2 * i + 1)) for i in range(len(q2_list))]


def _dup_f32_to_bf16(t):
    lo64 = lax.broadcasted_iota(jnp.int32, (1, LANES), 1) < HEAD_DIM
    return _dup_heads(t, lo64).astype(BF16)


def _gqa_heads(q, segs_of_group, sink_ref, o_ref):
    for g in range(N_KV_A):
        q2s = [q[:, (2 * g + i) * LANES:(2 * g + i + 1) * LANES] for i in range(2)]
        sinks = [sink_ref[4 * g + i] for i in range(4)]
        outs = _attend(q2s, segs_of_group(g), sinks)
        for i in range(2):
            o_ref[:, (2 * g + i) * LANES:(2 * g + i + 1) * LANES] = outs[i].astype(o_ref.dtype)


def _ctx_attn_a_kernel(sink_ref, q_ref, k_ref, v_ref, o_ref):
    q = q_ref[...]
    k = k_ref[...]
    v = v_ref[...]
    tile = lambda t, g: t[:, g * LANES:(g + 1) * LANES]
    _gqa_heads(q, lambda g: [(tile(k, g), tile(v, g), None)], sink_ref, o_ref)


def _ctx_attn_b_kernel(q_ref, k_ref, v_ref, o_ref):
    q = q_ref[...]
    k = k_ref[...].astype(BF16)
    v = v_ref[...].astype(BF16)
    for j in range(N_HEADS // 2):
        sl = slice(j * LANES, (j + 1) * LANES)
        (o2,) = _attend([q[:, sl]], [(k[:, sl], v[:, sl], None)], None)
        o_ref[:, sl] = o2.astype(o_ref.dtype)


def _ctx_attn_call(q, k, v, sink):
    b_n, l_n, nq = q.shape
    bmap = lambda b: (b, 0, 0)
    in_specs = [pl.BlockSpec((None, l_n, nq), bmap),
                pl.BlockSpec((None, l_n, k.shape[2]), bmap),
                pl.BlockSpec((None, l_n, v.shape[2]), bmap)]
    args = [q, k, v]
    kern = _ctx_attn_b_kernel
    if sink is not None:
        in_specs = [pl.BlockSpec(memory_space=pltpu.SMEM)] + in_specs
        args = [sink] + args
        kern = _ctx_attn_a_kernel
    return pl.pallas_call(
        kern,
        grid=(b_n,),
        in_specs=in_specs,
        out_specs=pl.BlockSpec((None, l_n, nq), bmap),
        out_shape=jax.ShapeDtypeStruct((b_n, l_n, nq), BF16),
        compiler_params=_cparams(1),
        name="ctx_attn",
    )(*args)


def _win_attn_kernel(sink_ref, q_ref, kp_ref, kc_ref, kn_ref, vp_ref, vc_ref, vn_ref,
                     ck_ref, cv_ref, o_ref):
    i = pl.program_id(1)
    nb = pl.num_programs(1)
    blk = q_ref.shape[0]
    q = q_ref[...]
    k_loc = jnp.concatenate([kp_ref[...], kc_ref[...], kn_ref[...]], axis=0)
    v_loc = jnp.concatenate([vp_ref[...], vc_ref[...], vn_ref[...]], axis=0)
    k_ctx = _dup_f32_to_bf16(ck_ref[...])
    v_ctx = _dup_f32_to_bf16(cv_ref[...])
    a = lax.broadcasted_iota(jnp.int32, (4 * blk, 3 * blk), 0) % blk
    j = lax.broadcasted_iota(jnp.int32, (4 * blk, 3 * blk), 1)
    valid = (j >= a) & (j <= a + 2 * WINDOW)
    valid &= (j >= blk) | (i > 0)
    valid &= (j < 2 * blk) | (i < nb - 1)
    bias = jnp.where(valid, 0.0, NEG).astype(F32)
    tile = lambda t, g: t[:, g * LANES:(g + 1) * LANES]
    _gqa_heads(q, lambda g: [(tile(k_loc, g), tile(v_loc, g), bias),
                             (tile(k_ctx, g), tile(v_ctx, g), None)], sink_ref, o_ref)


def _win_attn_call(q, kd, vd, ck, cv, sink):
    b_n, l_n, nq = q.shape
    blk = WINDOW
    nb = l_n // blk
    wkv = kd.shape[2]
    prev = lambda b, i: (b, jnp.maximum(i - 1, 0), 0)
    cur = lambda b, i: (b, i, 0)
    nxt = lambda b, i: (b, jnp.minimum(i + 1, nb - 1), 0)
    kv_specs = [pl.BlockSpec((None, blk, wkv), m) for m in (prev, cur, nxt)]
    ctx_spec = pl.BlockSpec((None,) + ck.shape[1:], lambda b, i: (b, 0, 0))
    return pl.pallas_call(
        _win_attn_kernel,
        grid=(b_n, nb),
        in_specs=[pl.BlockSpec(memory_space=pltpu.SMEM),
                  pl.BlockSpec((None, blk, nq), cur)] + kv_specs + kv_specs + [ctx_spec, ctx_spec],
        out_specs=pl.BlockSpec((None, blk, nq), cur),
        out_shape=jax.ShapeDtypeStruct((b_n, l_n, nq), BF16),
        compiler_params=_cparams(2),
        name="win_attn",
    )(sink, q, kd, kd, kd, vd, vd, vd, ck, cv)


def _nbr_window_start(r0, rows):
    return jnp.clip(r0 - WIN_R // 2, 0, rows - NBR_WIN)


def _nbr_pattern(r0, rows):
    ws = min(max(r0 - WIN_R // 2, 0), rows - NBR_WIN)
    pat = []
    for i in range(NBR_ROWS):
        r = r0 + i
        rs = min(max(r - WIN_R // 2, 0), rows - WIN_R)
        pat.append(tuple((ws + w - r + WIN_R - 1) if rs <= ws + w < rs + WIN_R else None
                         for w in range(NBR_WIN)))
    return tuple(pat)


def _nbr_class(rb, n_rb):
    return jnp.where(rb == 0, 0, jnp.where(rb == n_rb - 1, 2, 1))


def _nbr_attn_kernel(q_ref, k_ref, v_ref, bias_ref, ck_ref, cv_ref, o_ref, kc_ref, vc_ref):
    @pl.when(pl.program_id(1) == 0)
    def _():
        kc_ref[...] = ck_ref[...].astype(BF16)
        vc_ref[...] = cv_ref[...].astype(BF16)

    q = q_ref[...]
    for j in range(N_HEADS // 2):
        sl = slice(j * LANES, (j + 1) * LANES)
        bias = jnp.concatenate([bias_ref[2 * j], bias_ref[2 * j + 1]], axis=0)
        (o2,) = _attend([q[:, sl]], [(k_ref[0, :, sl], v_ref[0, :, sl], bias),
                                     (kc_ref[:, sl], vc_ref[:, sl], None)], None)
        o_ref[:, sl] = o2.astype(o_ref.dtype)


def _nbr_attn_call(q, k, v, bias_tab, ck, cv):
    b_n, l_n, nq = q.shape
    rows = l_n // GRID_W
    n_rb = rows // NBR_ROWS
    interior = _nbr_pattern(NBR_ROWS, rows)
    assert all(_nbr_pattern(rb * NBR_ROWS, rows) == interior for rb in range(1, n_rb - 1))
    mq = NBR_ROWS * GRID_W
    nk = NBR_WIN * GRID_W
    blk = lambda b, rb: (b, rb, 0)
    ctx_spec = pl.BlockSpec((None,) + ck.shape[1:], lambda b, rb: (b, 0, 0))
    kv_spec = pl.BlockSpec(
        (pl.Element(1), pl.Element(nk), pl.Element(nq)),
        lambda b, rb: (b, _nbr_window_start(rb * NBR_ROWS, rows) * GRID_W, 0))
    bias_spec = pl.BlockSpec((None, N_HEADS, mq, nk),
                             lambda b, rb: (_nbr_class(rb, n_rb), 0, 0, 0),
                             pipeline_mode=RESIDENT)
    return pl.pallas_call(
        _nbr_attn_kernel,
        grid=(b_n, n_rb),
        in_specs=[pl.BlockSpec((None, mq, nq), blk), kv_spec, kv_spec, bias_spec,
                  ctx_spec, ctx_spec],
        out_specs=pl.BlockSpec((None, mq, nq), blk),
        out_shape=jax.ShapeDtypeStruct((b_n, l_n, nq), BF16),
        scratch_shapes=[pltpu.VMEM(ck.shape[1:], BF16)] * 2,
        compiler_params=_cparams(2),
        name="nbr_attn",
    )(q, k, v, bias_tab, ck, cv)


def _block_ffn_kernel(x_ref, xp_ref, xn_ref, o_ref, op_ref, on_ref, mod_ref, wo_ref, nw_ref,
                      wup_ref, cw_ref, cb_ref, wdn_ref, y_ref, *u_refs, tiles_per_seq):
    tm = x_ref.shape[0]
    i = pl.program_id(1) % tiles_per_seq
    gate_attn = mod_ref[:, 2 * D_MODEL:3 * D_MODEL]
    shift = mod_ref[:, 3 * D_MODEL:4 * D_MODEL]
    scale = mod_ref[:, 4 * D_MODEL:5 * D_MODEL]
    gate_ffn = mod_ref[:, 5 * D_MODEL:6 * D_MODEL]
    x_ext = jnp.concatenate([xp_ref[...], x_ref[...], xn_ref[...]], axis=0)
    o_ext = jnp.concatenate([op_ref[...], o_ref[...], on_ref[...]], axis=0)
    x1_ext = x_ext + gate_attn * jnp.dot(o_ext, wo_ref[...], preferred_element_type=F32)
    x1 = x1_ext[HALO:HALO + tm]
    row = lax.broadcasted_iota(jnp.int32, (tm + 2 * HALO, 1), 0)
    inside = ((row >= HALO) | (i > 0)) & ((row < HALO + tm) | (i < tiles_per_seq - 1))
    h = jnp.where(inside, _rms_modulate(x1_ext, nw_ref[...], shift, scale), 0.0).astype(BF16)

    n_slabs = FF_CHUNK // LANES

    def conv(u_ref, col):
        outs = []
        for s in range(n_slabs):
            lanes = slice(col + s * LANES, col + (s + 1) * LANES)
            cw = cw_ref[:, lanes]
            out = cb_ref[:, lanes]
            for o in range(3):
                out = out + u_ref[s, pl.ds(HALO - 1 + o, tm), :] * cw[o:o + 1, :]
            outs.append(out)
        return jnp.concatenate(outs, axis=1)

    def up_one(u_ref, col):
        u = jnp.dot(h, wup_ref[:, col:col + FF_CHUNK], preferred_element_type=F32)
        for s in range(n_slabs):
            u_ref[s] = u[:, s * LANES:(s + 1) * LANES]

    def up(c):
        up_one(u_refs[2 * (c % FF_SLOTS)], c * FF_CHUNK)
        up_one(u_refs[2 * (c % FF_SLOTS) + 1], D_FF + c * FF_CHUNK)

    n_chunks = D_FF // FF_CHUNK
    acc = jnp.zeros((tm, D_MODEL), F32)
    up(0)
    for c in range(n_chunks):
        if c + 1 < n_chunks:
            up(c + 1)
        ug_ref, uv_ref = u_refs[2 * (c % FF_SLOTS)], u_refs[2 * (c % FF_SLOTS) + 1]
        gc = c * FF_CHUNK
        act = (_silu(conv(ug_ref, gc)) * conv(uv_ref, D_FF + gc)).astype(BF16)
        acc = acc + jnp.dot(act, wdn_ref[gc:gc + FF_CHUNK, :], preferred_element_type=F32)
    y_ref[...] = x1 + gate_ffn * acc


def _block_ffn_call(x, o, mod4, layer, row0, w_o, nw, w_up, conv_w, conv_b, w_down, *,
                    seq_len, tm):
    g_n, l_n, _ = x.shape
    tiles_per_seq = seq_len // tm
    nh = l_n // HALO
    hb = tm // HALO
    const = lambda g, i: (0, 0)
    tile = lambda g, i: (g, i, 0)
    prev = lambda g, i: (g, jnp.maximum(i * hb - 1, 0), 0)
    nxt = lambda g, i: (g, jnp.minimum((i + 1) * hb, nh - 1), 0)
    tile_specs = [pl.BlockSpec((None, tm, D_MODEL), tile),
                  pl.BlockSpec((None, HALO, D_MODEL), prev),
                  pl.BlockSpec((None, HALO, D_MODEL), nxt)]
    return pl.pallas_call(
        functools.partial(_block_ffn_kernel, tiles_per_seq=tiles_per_seq),
        grid=(g_n, l_n // tm),
        in_specs=tile_specs + tile_specs + [
            pl.BlockSpec((None, None, 1, 6 * D_MODEL), lambda g, i: (layer, row0 + g, 0, 0)),
            pl.BlockSpec(w_o.shape, const, pipeline_mode=RESIDENT),
            pl.BlockSpec((1, D_MODEL), const),
            pl.BlockSpec(w_up.shape, const, pipeline_mode=RESIDENT),
            pl.BlockSpec(conv_w.shape, const),
            pl.BlockSpec(conv_b.shape, const),
            pl.BlockSpec(w_down.shape, const, pipeline_mode=RESIDENT),
        ],
        out_specs=pl.BlockSpec((None, tm, D_MODEL), tile),
        out_shape=jax.ShapeDtypeStruct(x.shape, F32),
        scratch_shapes=[pltpu.VMEM((FF_CHUNK // LANES, tm + 2 * HALO, LANES), F32)]
        * (2 * FF_SLOTS),
        compiler_params=_cparams(2),
        name="proj_conv_ffn",
    )(x, x, x, o, o, o, mod4, w_o, nw, w_up, conv_w, conv_b, w_down)


def _rope_tables(l_n):
    half = HEAD_DIM // 4
    freqs = ROPE_BASE ** (-jnp.arange(half, dtype=F32) / half)
    t = jnp.arange(l_n)
    lane = jnp.arange(HEAD_DIM)
    pos = jnp.where(lane[None, :] < HEAD_DIM // 2, (t // GRID_W)[:, None], (t % GRID_W)[:, None])
    ang = pos.astype(F32) * freqs[lane % half][None, :]
    sign = jnp.where((lane % (2 * half)) < half, -1.0, 1.0).astype(F32)
    cos = jnp.cos(ang)
    sin = jnp.sin(ang) * sign[None, :]
    return jnp.tile(cos, (1, 2)), jnp.tile(sin, (1, 2))


def _nbr_bias_kernel(rpb_ref, o_ref, *, patterns):
    n_dr = 2 * WIN_R - 1
    n_dc = 2 * WIN_C - 1
    base = pl.program_id(0) * (n_dr * n_dc)
    qc = lax.broadcasted_iota(jnp.int32, (GRID_W, LANES), 0)
    lane = lax.broadcasted_iota(jnp.int32, (GRID_W, LANES), 1)
    kc = lane % GRID_W
    dc = jnp.clip(kc - qc, -(WIN_C - 1), WIN_C - 1) + WIN_C - 1
    cs = jnp.clip(qc - WIN_C // 2, 0, GRID_W - WIN_C)
    col_ok = (kc >= cs) & (kc < cs + WIN_C)
    masked = jnp.full((GRID_W, LANES), NEG, F32)
    blocks = []
    for a in range(n_dr):
        t = masked
        for d in range(n_dc):
            t = jnp.where(col_ok & (dc == d), rpb_ref[base + a * n_dc + d], t)
        blocks.append(t)
    pick = lambda a: masked if a is None else blocks[a]
    lo = lane < GRID_W
    for cls, pat in enumerate(patterns):
        for i in range(NBR_ROWS):
            for m in range(NBR_WIN // 2):
                o_ref[cls, i * GRID_W:(i + 1) * GRID_W, m * LANES:(m + 1) * LANES] = jnp.where(
                    lo, pick(pat[i][2 * m]), pick(pat[i][2 * m + 1]))


def _nbr_bias_table(rpb, rows):
    n_heads = rpb.shape[0]
    patterns = tuple(_nbr_pattern(r0, rows) for r0 in (0, NBR_ROWS, rows - NBR_ROWS))
    mq = NBR_ROWS * GRID_W
    nk = NBR_WIN * GRID_W
    return pl.pallas_call(
        functools.partial(_nbr_bias_kernel, patterns=patterns),
        grid=(n_heads,),
        in_specs=[pl.BlockSpec(memory_space=pltpu.SMEM)],
        out_specs=pl.BlockSpec((len(patterns), None, mq, nk), lambda h: (0, h, 0, 0)),
        out_shape=jax.ShapeDtypeStruct((len(patterns), n_heads, mq, nk), F32),
        compiler_params=_cparams(1),
        name="nbr_bias",
    )(rpb.astype(F32).reshape(-1))


def _block_diag_mean():
    r = jnp.arange(NORM_CHUNK) // HEAD_DIM
    return jnp.where(r[:, None] == r[None, :], 1.0 / HEAD_DIM, 0.0).astype(BF16)


def kernel(x_prompt, x_sample, cache_k_a, cache_v_a, cache_k_b, cache_v_b, c, c_ctx,
           norm_attn_w, norm_ffn_w, w_ada, b_ada,
           w_qkv_a, q_norm_a, k_norm_a, sink_a, w_o_a,
           w_qkv_b, q_norm_b, k_norm_b, rpb_b, w_o_b,
           w_up, conv_w, conv_b, w_down):
    depth = w_ada.shape[0]
    batch, seq, _ = x_prompt.shape
    dec_batch, dec_seq, _ = x_sample.shape
    nq = N_HEADS * HEAD_DIM

    cond = jnp.concatenate(
        [c_ctx[None, :], c, jnp.zeros((N_MOD_ROWS - 1 - dec_batch, D_MODEL), F32)], axis=0)
    mod = _ada_call(cond, w_ada, b_ada[:, None, :])
    mod4 = mod.reshape(depth, N_MOD_ROWS, 1, 6 * D_MODEL)

    bd = _block_diag_mean()
    rope_tabs = _rope_tables(dec_seq)
    tile_w = lambda w, n: jnp.tile(w, n)[None, :]

    xp = x_prompt.reshape(1, batch * seq, D_MODEL)
    xs = x_sample
    new_k_a, new_v_a, new_k_b, new_v_b = [], [], [], []
    for i in range(depth):
        j = i // 2
        nw_attn = norm_attn_w[i][None, :]
        nw_ffn = norm_ffn_w[i][None, :]
        if i % 2 == 0:
            nkv = N_KV_A * HEAD_DIM
            w_qkv = w_qkv_a[j].astype(BF16)
            w_o = w_o_a[j].astype(BF16)
            qn, kn = tile_w(q_norm_a[j], N_HEADS), tile_w(k_norm_a[j], N_KV_A)
            q, k, v, kd, vd = _qkv_call(xp, mod4, i, 0, nw_attn, w_qkv, qn, kn, bd, None,
                                        nkv=nkv, dup=True, f32_kv=True, tm=512)
            new_k_a.append(k.reshape(batch, seq, N_KV_A, HEAD_DIM))
            new_v_a.append(v.reshape(batch, seq, N_KV_A, HEAD_DIM))
            shp = (batch, seq, -1)
            op = _ctx_attn_call(q.reshape(shp), kd.reshape(shp), vd.reshape(shp), sink_a[j])
            q, kd, vd = _qkv_call(xs, mod4, i, 1, nw_attn, w_qkv, qn, kn, bd, rope_tabs,
                                  nkv=nkv, dup=True, f32_kv=False, tm=512)
            o_s = _win_attn_call(q, kd, vd,
                                 cache_k_a[:, j].reshape(dec_batch, -1, nkv),
                                 cache_v_a[:, j].reshape(dec_batch, -1, nkv), sink_a[j])
        else:
            nkv = nq
            w_qkv = w_qkv_b[j].astype(BF16)
            w_o = w_o_b[j].astype(BF16)
            qn, kn = tile_w(q_norm_b[j], N_HEADS), tile_w(k_norm_b[j], N_HEADS)
            q, k, v = _qkv_call(xp, mod4, i, 0, nw_attn, w_qkv, qn, kn, bd, None,
                                nkv=nkv, dup=False, f32_kv=True, tm=512)
            new_k_b.append(k.reshape(batch, seq, N_HEADS, HEAD_DIM))
            new_v_b.append(v.reshape(batch, seq, N_HEADS, HEAD_DIM))
            shp = (batch, seq, -1)
            op = _ctx_attn_call(q.reshape(shp), k.reshape(shp), v.reshape(shp), None)
            q, k, v = _qkv_call(xs, mod4, i, 1, nw_attn, w_qkv, qn, kn, bd, None,
                                nkv=nkv, dup=False, f32_kv=False, tm=512)
            o_s = _nbr_attn_call(q, k, v, _nbr_bias_table(rpb_b[j], dec_seq // GRID_W),
                                 cache_k_b[:, j].reshape(dec_batch, -1, nq),
                                 cache_v_b[:, j].reshape(dec_batch, -1, nq))
        blk_w = (w_o, nw_ffn, w_up[i].astype(BF16), conv_w[i], conv_b[i][None, :],
                 w_down[i].astype(BF16))
        xp = _block_ffn_call(xp, op.reshape(xp.shape), mod4, i, 0, *blk_w, seq_len=seq, tm=seq)
        xs = _block_ffn_call(xs, o_s, mod4, i, 1, *blk_w, seq_len=dec_seq, tm=512)
    return (xp.reshape(batch, seq, D_MODEL), xs,
            jnp.stack(new_k_a, axis=1), jnp.stack(new_v_a, axis=1),
            jnp.stack(new_k_b, axis=1), jnp.stack(new_v_b, axis=1))
```

```python
import functools

import jax
import jax.numpy as jnp
from jax import lax
from jax.experimental import pallas as pl
from jax.experimental.pallas import tpu as pltpu

D_MODEL = 1024
HEAD_DIM = 64
N_HEADS = 16
N_KV_A = 4
GRID_W = 64
WINDOW = 128
WIN_R = 8
WIN_C = 16
D_FF = 2816
ROPE_BASE = 10000.0
EPS = 1e-6
SCALE = HEAD_DIM ** -0.5

LANES = 128
HALO = 16
NBR_ROWS = 4
NBR_WIN = NBR_ROWS + WIN_R
NORM_CHUNK = 256
FF_CHUNK = 256
FF_SLOTS = 3
NEG = -1e30
N_MOD_ROWS = 8

RESIDENT = pl.Buffered(1)

F32 = jnp.float32
BF16 = jnp.bfloat16
VMEM_LIMIT = 56 * 1024 * 1024


def _cparams(n_axes):
    return pltpu.CompilerParams(
        dimension_semantics=("arbitrary",) * n_axes, vmem_limit_bytes=VMEM_LIMIT)


def _silu(x):
    return x * (1.0 / (1.0 + jnp.exp(-x)))


def _rms_modulate(x, nw, shift, scale):
    ms = jnp.mean(x * x, axis=-1, keepdims=True)
    y = x * lax.rsqrt(ms + EPS) * nw
    return y * (1.0 + scale) + shift


def _ada_kernel(c_ref, w_ref, b_ref, o_ref):
    a = _silu(c_ref[...])
    o_ref[...] = jnp.dot(a, w_ref[...], precision=lax.Precision.HIGHEST,
                         preferred_element_type=F32) + b_ref[...]


def _ada_call(cond, w_ada, b_ada):
    depth = w_ada.shape[0]
    tn = 1024
    return pl.pallas_call(
        _ada_kernel,
        grid=(depth, 6 * D_MODEL // tn),
        in_specs=[
            pl.BlockSpec((N_MOD_ROWS, D_MODEL), lambda l, n: (0, 0)),
            pl.BlockSpec((None, D_MODEL, tn), lambda l, n: (l, 0, n)),
            pl.BlockSpec((None, 1, tn), lambda l, n: (l, 0, n)),
        ],
        out_specs=pl.BlockSpec((None, N_MOD_ROWS, tn), lambda l, n: (l, 0, n)),
        out_shape=jax.ShapeDtypeStruct((depth, N_MOD_ROWS, 6 * D_MODEL), F32),
        compiler_params=_cparams(2),
        name="adaln",
    )(cond, w_ada, b_ada)


def _head_rms(t, bd, w):
    outs = []
    for c in range(t.shape[1] // NORM_CHUNK):
        tc = t[:, c * NORM_CHUNK:(c + 1) * NORM_CHUNK]
        ms = jnp.dot((tc * tc).astype(BF16), bd, preferred_element_type=F32)
        outs.append(tc * lax.rsqrt(ms + EPS))
    return jnp.concatenate(outs, axis=1) * w


def _rope(t, cos, sin_signed, lo16):
    outs = []
    for c in range(t.shape[1] // LANES):
        tc = t[:, c * LANES:(c + 1) * LANES]
        partner = jnp.where(lo16, pltpu.roll(tc, LANES - 16, axis=1), pltpu.roll(tc, 16, axis=1))
        outs.append(tc * cos + partner * sin_signed)
    return jnp.concatenate(outs, axis=1)


def _dup_heads(t, lo64):
    outs = []
    for c in range(t.shape[1] // LANES):
        tc = t[:, c * LANES:(c + 1) * LANES]
        sw = pltpu.roll(tc, HEAD_DIM, axis=1)
        outs.append(jnp.where(lo64, tc, sw))
        outs.append(jnp.where(lo64, sw, tc))
    return jnp.concatenate(outs, axis=1)


def _qkv_kernel(*refs, nkv, rope, dup, f32_kv):
    x_ref, mod_ref, nw_ref, w_ref, qn_ref, kn_ref, bd_ref = refs[:7]
    pos = 7
    if rope:
        cos_ref, sin_ref = refs[pos:pos + 2]
        pos += 2
    outs = refs[pos:]
    nq = N_HEADS * HEAD_DIM
    lane = lax.broadcasted_iota(jnp.int32, (1, LANES), 1)

    h = _rms_modulate(x_ref[...], nw_ref[...], mod_ref[:, 0:D_MODEL],
                      mod_ref[:, D_MODEL:2 * D_MODEL])
    qkv = jnp.dot(h.astype(BF16), w_ref[...], preferred_element_type=F32)
    bd = bd_ref[...]
    q = _head_rms(qkv[:, :nq], bd, qn_ref[...])
    k = _head_rms(qkv[:, nq:nq + nkv], bd, kn_ref[...])
    v = qkv[:, nq + nkv:]
    if rope:
        lo16 = (lane & 16) == 0
        q = _rope(q, cos_ref[...], sin_ref[...], lo16)
        k = _rope(k, cos_ref[...], sin_ref[...], lo16)
    outs[0][...] = (q * SCALE).astype(BF16)
    o = 1
    if f32_kv:
        outs[o][...] = k
        outs[o + 1][...] = v
        o += 2
    if dup:
        lo64 = lane < HEAD_DIM
        k = _dup_heads(k, lo64)
        v = _dup_heads(v, lo64)
    if dup or not f32_kv:
        outs[o][...] = k.astype(BF16)
        outs[o + 1][...] = v.astype(BF16)


def _qkv_call(x, mod4, layer, row0, nw, w, qn, kn, bd, rope_tabs, *, nkv, dup, f32_kv, tm):
    g_n, l_n, _ = x.shape
    rope = rope_tabs is not None
    nq = N_HEADS * HEAD_DIM
    xmap = lambda g, i: (g, i, 0)
    const = lambda g, i: (0, 0)
    in_specs = [
        pl.BlockSpec((None, tm, D_MODEL), xmap),
        pl.BlockSpec((None, None, 1, 6 * D_MODEL), lambda g, i: (layer, row0 + g, 0, 0)),
        pl.BlockSpec((1, D_MODEL), const),
        pl.BlockSpec(w.shape, const),
        pl.BlockSpec((1, nq), const),
        pl.BlockSpec((1, nkv), const),
        pl.BlockSpec((NORM_CHUNK, NORM_CHUNK), const),
    ]
    args = [x, mod4, nw, w, qn, kn, bd]
    if rope:
        in_specs += [pl.BlockSpec((tm, LANES), lambda g, i: (i, 0))] * 2
        args += list(rope_tabs)
    out_shape = [jax.ShapeDtypeStruct((g_n, l_n, nq), BF16)]
    out_specs = [pl.BlockSpec((None, tm, nq), xmap)]
    if f32_kv:
        out_shape += [jax.ShapeDtypeStruct((g_n, l_n, nkv), F32)] * 2
        out_specs += [pl.BlockSpec((None, tm, nkv), xmap)] * 2
    if dup or not f32_kv:
        wkv = nkv * 2 if dup else nkv
        out_shape += [jax.ShapeDtypeStruct((g_n, l_n, wkv), BF16)] * 2
        out_specs += [pl.BlockSpec((None, tm, wkv), xmap)] * 2
    return pl.pallas_call(
        functools.partial(_qkv_kernel, nkv=nkv, rope=rope, dup=dup, f32_kv=f32_kv),
        grid=(g_n, l_n // tm),
        in_specs=in_specs,
        out_specs=out_specs,
        out_shape=out_shape,
        compiler_params=_cparams(2),
        name="qkv",
    )(*args)


def _scores(q2_list, segs):
    lo = lax.broadcasted_iota(jnp.int32, (1, LANES), 1) < HEAD_DIM
    zero = jnp.zeros((), BF16)
    stacked = []
    for q2 in q2_list:
        stacked.append(jnp.where(lo, q2, zero))
        stacked.append(jnp.where(lo, zero, q2))
    qs = jnp.concatenate(stacked, axis=0)
    scores = []
    for k2, _, bias in segs:
        s = lax.dot_general(qs, k2, (((1,), (1,)), ((), ())), preferred_element_type=F32)
        scores.append(s if bias is None else s + bias)
    return scores


def _softmax_pv(scores, segs, sinks, m_rows):
    lo = lax.broadcasted_iota(jnp.int32, (1, LANES), 1) < HEAD_DIM
    mx = functools.reduce(jnp.maximum, [jnp.max(s, axis=-1, keepdims=True) for s in scores])
    if sinks is not None:
        sink_col = jnp.concatenate(
            [jnp.full((m_rows, 1), sk, F32) for sk in sinks], axis=0)
        mx = jnp.maximum(mx, sink_col)
    den = jnp.exp(sink_col - mx) if sinks is not None else jnp.zeros_like(mx)
    acc = jnp.zeros((scores[0].shape[0], LANES), F32)
    for s, (_, v2, _) in zip(scores, segs):
        p = jnp.exp(s - mx)
        den = den + jnp.sum(p, axis=-1, keepdims=True)
        acc = acc + jnp.dot(p.astype(BF16), v2, preferred_element_type=F32)
    rows = lambda t, n: t[n * m_rows:(n + 1) * m_rows]
    out = acc / den
    return [jnp.where(lo, rows(out, 2 * i), rows(out, 2 * i + 1))
            for i in range(out.shape[0] // (2 * m_rows))]


def _run_attention(jobs, lookahead=True):
    scores = _scores(*jobs[0][:2])
    for n, (q2_list, segs, sinks, store) in enumerate(jobs):
        more = n + 1 < len(jobs)
        if lookahead and more:
            nxt = _scores(*jobs[n + 1][:2])
        store(_softmax_pv(scores, segs, sinks, q2_list[0].shape[0]))
        if more:
            scores = nxt if lookahead else _scores(*jobs[n + 1][:2])


def _pair_slices():
    return [slice(j * LANES, (j + 1) * LANES) for j in range(N_HEADS // 2)]


def _pair_store(o_ref, sl):
    def _store(outs):
        o_ref[:, sl] = outs[0].astype(o_ref.dtype)
    return _store


def _dup_f32_to_bf16(t):
    lo64 = lax.broadcasted_iota(jnp.int32, (1, LANES), 1) < HEAD_DIM
    return _dup_heads(t, lo64).astype(BF16)


def _gqa_heads(q, segs_of_group, sink_ref, o_ref, lookahead):
    pairs_per_job = 2
    def store(pairs):
        def _store(outs):
            for p, out in zip(pairs, outs):
                o_ref[:, p * LANES:(p + 1) * LANES] = out.astype(o_ref.dtype)
        return _store

    jobs = []
    for g in range(N_KV_A):
        for first in range(2 * g, 2 * g + 2, pairs_per_job):
            pairs = list(range(first, first + pairs_per_job))
            jobs.append(([q[:, p * LANES:(p + 1) * LANES] for p in pairs], segs_of_group(g),
                         [sink_ref[2 * p + i] for p in pairs for i in range(2)], store(pairs)))
    _run_attention(jobs, lookahead)


def _ctx_attn_a_kernel(sink_ref, q_ref, k_ref, v_ref, o_ref):
    q = q_ref[...]
    k = k_ref[...]
    v = v_ref[...]
    tile = lambda t, g: t[:, g * LANES:(g + 1) * LANES]
    _gqa_heads(q, lambda g: [(tile(k, g), tile(v, g), None)], sink_ref, o_ref, True)


def _ctx_attn_b_kernel(q_ref, k_ref, v_ref, o_ref):
    q = q_ref[...]
    k = k_ref[...].astype(BF16)
    v = v_ref[...].astype(BF16)
    _run_attention([
        ([q[:, sl]], [(k[:, sl], v[:, sl], None)], None, _pair_store(o_ref, sl))
        for sl in _pair_slices()])


def _ctx_attn_call(q, k, v, sink):
    b_n, l_n, nq = q.shape
    bmap = lambda b: (b, 0, 0)
    in_specs = [pl.BlockSpec((None, l_n, nq), bmap),
                pl.BlockSpec((None, l_n, k.shape[2]), bmap),
                pl.BlockSpec((None, l_n, v.shape[2]), bmap)]
    args = [q, k, v]
    kern = _ctx_attn_b_kernel
    if sink is not None:
        in_specs = [pl.BlockSpec(memory_space=pltpu.SMEM)] + in_specs
        args = [sink] + args
        kern = _ctx_attn_a_kernel
    return pl.pallas_call(
        kern,
        grid=(b_n,),
        in_specs=in_specs,
        out_specs=pl.BlockSpec((None, l_n, nq), bmap),
        out_shape=jax.ShapeDtypeStruct((b_n, l_n, nq), BF16),
        compiler_params=_cparams(1),
        name="ctx_attn",
    )(*args)


def _win_attn_kernel(sink_ref, q_ref, kp_ref, kc_ref, kn_ref, vp_ref, vc_ref, vn_ref,
                     ck_ref, cv_ref, o_ref, kctx_ref, vctx_ref, bias_ref):
    i = pl.program_id(1)
    nb = pl.num_programs(1)
    blk = q_ref.shape[0]

    @pl.when(i == 0)
    def _():
        kctx_ref[...] = _dup_f32_to_bf16(ck_ref[...])
        vctx_ref[...] = _dup_f32_to_bf16(cv_ref[...])
        a = lax.broadcasted_iota(jnp.int32, bias_ref.shape[1:], 0) % blk
        j = lax.broadcasted_iota(jnp.int32, bias_ref.shape[1:], 1)
        band = (j >= a) & (j <= a + 2 * WINDOW)
        for cls, ok in enumerate((band & (j >= blk), band, band & (j < 2 * blk))):
            bias_ref[cls] = jnp.where(ok, 0.0, NEG).astype(F32)

    q = q_ref[...]
    k_loc = jnp.concatenate([kp_ref[...], kc_ref[...], kn_ref[...]], axis=0)
    v_loc = jnp.concatenate([vp_ref[...], vc_ref[...], vn_ref[...]], axis=0)
    bias = bias_ref[jnp.where(i == 0, 0, jnp.where(i == nb - 1, 2, 1))]
    tile = lambda t, g: t[:, g * LANES:(g + 1) * LANES]
    _gqa_heads(q, lambda g: [(tile(k_loc, g), tile(v_loc, g), bias),
                             (kctx_ref[:, g * LANES:(g + 1) * LANES],
                              vctx_ref[:, g * LANES:(g + 1) * LANES], None)],
               sink_ref, o_ref, False)


def _win_attn_call(q, kd, vd, ck, cv, sink):
    b_n, l_n, nq = q.shape
    blk = WINDOW
    nb = l_n // blk
    assert nb >= 3
    wkv = kd.shape[2]
    prev = lambda b, i: (b, jnp.maximum(i - 1, 0), 0)
    cur = lambda b, i: (b, i, 0)
    nxt = lambda b, i: (b, jnp.minimum(i + 1, nb - 1), 0)
    kv_specs = [pl.BlockSpec((None, blk, wkv), m) for m in (prev, cur, nxt)]
    ctx_spec = pl.BlockSpec((None,) + ck.shape[1:], lambda b, i: (b, 0, 0))
    return pl.pallas_call(
        _win_attn_kernel,
        grid=(b_n, nb),
        in_specs=[pl.BlockSpec(memory_space=pltpu.SMEM),
                  pl.BlockSpec((None, blk, nq), cur)] + kv_specs + kv_specs + [ctx_spec, ctx_spec],
        out_specs=pl.BlockSpec((None, blk, nq), cur),
        out_shape=jax.ShapeDtypeStruct((b_n, l_n, nq), BF16),
        scratch_shapes=[pltpu.VMEM((ck.shape[1], wkv), BF16)] * 2
        + [pltpu.VMEM((3, N_HEADS // N_KV_A * blk, 3 * blk), F32)],
        compiler_params=_cparams(2),
        name="win_attn",
    )(sink, q, kd, kd, kd, vd, vd, vd, ck, cv)


def _nbr_window_start(r0, rows):
    return jnp.clip(r0 - WIN_R // 2, 0, rows - NBR_WIN)


def _nbr_pattern(r0, rows):
    ws = min(max(r0 - WIN_R // 2, 0), rows - NBR_WIN)
    pat = []
    for i in range(NBR_ROWS):
        r = r0 + i
        rs = min(max(r - WIN_R // 2, 0), rows - WIN_R)
        pat.append(tuple((ws + w - r + WIN_R - 1) if rs <= ws + w < rs + WIN_R else None
                         for w in range(NBR_WIN)))
    return tuple(pat)


def _nbr_class(rb, n_rb):
    return jnp.where(rb == 0, 0, jnp.where(rb == n_rb - 1, 2, 1))


def _nbr_attn_kernel(q_ref, k_ref, v_ref, bias_ref, ck_ref, cv_ref, o_ref, kc_ref, vc_ref):
    @pl.when(pl.program_id(1) == 0)
    def _():
        kc_ref[...] = ck_ref[...].astype(BF16)
        vc_ref[...] = cv_ref[...].astype(BF16)

    q = q_ref[...]
    _run_attention([
        ([q[:, sl]],
         [(k_ref[0, :, sl], v_ref[0, :, sl],
           jnp.concatenate([bias_ref[2 * j], bias_ref[2 * j + 1]], axis=0)),
          (kc_ref[:, sl], vc_ref[:, sl], None)],
         None, _pair_store(o_ref, sl))
        for j, sl in enumerate(_pair_slices())])


def _nbr_attn_call(q, k, v, bias_tab, ck, cv):
    b_n, l_n, nq = q.shape
    rows = l_n // GRID_W
    n_rb = rows // NBR_ROWS
    interior = _nbr_pattern(NBR_ROWS, rows)
    assert all(_nbr_pattern(rb * NBR_ROWS, rows) == interior for rb in range(1, n_rb - 1))
    mq = NBR_ROWS * GRID_W
    nk = NBR_WIN * GRID_W
    blk = lambda b, rb: (b, rb, 0)
    ctx_spec = pl.BlockSpec((None,) + ck.shape[1:], lambda b, rb: (b, 0, 0))
    kv_spec = pl.BlockSpec(
        (pl.Element(1), pl.Element(nk), pl.Element(nq)),
        lambda b, rb: (b, _nbr_window_start(rb * NBR_ROWS, rows) * GRID_W, 0))
    bias_spec = pl.BlockSpec((None, N_HEADS, mq, nk),
                             lambda b, rb: (_nbr_class(rb, n_rb), 0, 0, 0),
                             pipeline_mode=RESIDENT)
    return pl.pallas_call(
        _nbr_attn_kernel,
        grid=(b_n, n_rb),
        in_specs=[pl.BlockSpec((None, mq, nq), blk), kv_spec, kv_spec, bias_spec,
                  ctx_spec, ctx_spec],
        out_specs=pl.BlockSpec((None, mq, nq), blk),
        out_shape=jax.ShapeDtypeStruct((b_n, l_n, nq), BF16),
        scratch_shapes=[pltpu.VMEM(ck.shape[1:], BF16)] * 2,
        compiler_params=_cparams(2),
        name="nbr_attn",
    )(q, k, v, bias_tab, ck, cv)


def _block_ffn_kernel(*refs, tiles_per_seq):
    halo = tiles_per_seq > 1
    if halo:
        x_ref, xp_ref, xn_ref, o_ref, op_ref, on_ref = refs[:6]
        refs = refs[6:]
    else:
        x_ref, o_ref = refs[:2]
        refs = refs[2:]
    mod_ref, wo_ref, nw_ref, wup_ref, cw_ref, cb_ref, wdn_ref, y_ref = refs[:8]
    u_refs = refs[8:]
    tm = x_ref.shape[0]
    n_slabs = FF_CHUNK // LANES
    gate_attn = mod_ref[:, 2 * D_MODEL:3 * D_MODEL]
    shift = mod_ref[:, 3 * D_MODEL:4 * D_MODEL]
    scale = mod_ref[:, 4 * D_MODEL:5 * D_MODEL]
    gate_ffn = mod_ref[:, 5 * D_MODEL:6 * D_MODEL]
    if halo:
        i = pl.program_id(1) % tiles_per_seq
        x_ext = jnp.concatenate([xp_ref[...], x_ref[...], xn_ref[...]], axis=0)
        o_ext = jnp.concatenate([op_ref[...], o_ref[...], on_ref[...]], axis=0)
        x1_ext = x_ext + gate_attn * jnp.dot(o_ext, wo_ref[...], preferred_element_type=F32)
        x1 = x1_ext[HALO:HALO + tm]
        row = lax.broadcasted_iota(jnp.int32, (tm + 2 * HALO, 1), 0)
        inside = ((row >= HALO) | (i > 0)) & ((row < HALO + tm) | (i < tiles_per_seq - 1))
        h = jnp.where(inside, _rms_modulate(x1_ext, nw_ref[...], shift, scale), 0.0).astype(BF16)
        u_rows = slice(None)
    else:
        x1 = x_ref[...] + gate_attn * jnp.dot(o_ref[...], wo_ref[...],
                                              preferred_element_type=F32)
        h = _rms_modulate(x1, nw_ref[...], shift, scale).astype(BF16)
        u_rows = pl.ds(HALO, tm)
        pad = jnp.zeros((8, LANES), F32)
        for u_ref in u_refs:
            for s in range(n_slabs):
                u_ref[s, pl.ds(HALO - 8, 8), :] = pad
                u_ref[s, pl.ds(HALO + tm, 8), :] = pad

    def conv(u_ref, col):
        outs = []
        for s in range(n_slabs):
            lanes = slice(col + s * LANES, col + (s + 1) * LANES)
            cw = cw_ref[:, lanes]
            out = cb_ref[:, lanes]
            for o in range(3):
                out = out + u_ref[s, pl.ds(HALO - 1 + o, tm), :] * cw[o:o + 1, :]
            outs.append(out)
        return jnp.concatenate(outs, axis=1)

    def up_one(u_ref, col):
        u = jnp.dot(h, wup_ref[:, col:col + FF_CHUNK], preferred_element_type=F32)
        for s in range(n_slabs):
            u_ref[s, u_rows, :] = u[:, s * LANES:(s + 1) * LANES]

    def up(c):
        up_one(u_refs[2 * (c % FF_SLOTS)], c * FF_CHUNK)
        up_one(u_refs[2 * (c % FF_SLOTS) + 1], D_FF + c * FF_CHUNK)

    n_chunks = D_FF // FF_CHUNK
    acc = jnp.zeros((tm, D_MODEL), F32)
    up(0)
    for c in range(n_chunks):
        if c + 1 < n_chunks:
            up(c + 1)
        ug_ref, uv_ref = u_refs[2 * (c % FF_SLOTS)], u_refs[2 * (c % FF_SLOTS) + 1]
        gc = c * FF_CHUNK
        act = (_silu(conv(ug_ref, gc)) * conv(uv_ref, D_FF + gc)).astype(BF16)
        acc = acc + jnp.dot(act, wdn_ref[gc:gc + FF_CHUNK, :], preferred_element_type=F32)
    y_ref[...] = x1 + gate_ffn * acc


def _block_ffn_call(x, o, mod4, layer, row0, w_o, nw, w_up, conv_w, conv_b, w_down, *,
                    seq_len, tm):
    g_n, l_n, _ = x.shape
    tiles_per_seq = seq_len // tm
    nh = l_n // HALO
    hb = tm // HALO
    const = lambda g, i: (0, 0)
    tile = lambda g, i: (g, i, 0)
    prev = lambda g, i: (g, jnp.maximum(i * hb - 1, 0), 0)
    nxt = lambda g, i: (g, jnp.minimum((i + 1) * hb, nh - 1), 0)
    tile_specs = [pl.BlockSpec((None, tm, D_MODEL), tile)]
    x_args, o_args = [x], [o]
    if tiles_per_seq > 1:
        tile_specs += [pl.BlockSpec((None, HALO, D_MODEL), prev),
                       pl.BlockSpec((None, HALO, D_MODEL), nxt)]
        x_args, o_args = [x] * 3, [o] * 3
    return pl.pallas_call(
        functools.partial(_block_ffn_kernel, tiles_per_seq=tiles_per_seq),
        grid=(g_n, l_n // tm),
        in_specs=tile_specs + tile_specs + [
            pl.BlockSpec((None, None, 1, 6 * D_MODEL), lambda g, i: (layer, row0 + g, 0, 0)),
            pl.BlockSpec(w_o.shape, const, pipeline_mode=RESIDENT),
            pl.BlockSpec((1, D_MODEL), const),
            pl.BlockSpec(w_up.shape, const, pipeline_mode=RESIDENT),
            pl.BlockSpec(conv_w.shape, const),
            pl.BlockSpec(conv_b.shape, const),
            pl.BlockSpec(w_down.shape, const, pipeline_mode=RESIDENT),
        ],
        out_specs=pl.BlockSpec((None, tm, D_MODEL), tile),
        out_shape=jax.ShapeDtypeStruct(x.shape, F32),
        scratch_shapes=[pltpu.VMEM((FF_CHUNK // LANES, tm + 2 * HALO, LANES), F32)]
        * (2 * FF_SLOTS),
        compiler_params=_cparams(2),
        name="proj_conv_ffn",
    )(*x_args, *o_args, mod4, w_o, nw, w_up, conv_w, conv_b, w_down)


def _rope_tables(l_n):
    half = HEAD_DIM // 4
    freqs = ROPE_BASE ** (-jnp.arange(half, dtype=F32) / half)
    t = jnp.arange(l_n)
    lane = jnp.arange(HEAD_DIM)
    pos = jnp.where(lane[None, :] < HEAD_DIM // 2, (t // GRID_W)[:, None], (t % GRID_W)[:, None])
    ang = pos.astype(F32) * freqs[lane % half][None, :]
    sign = jnp.where((lane % (2 * half)) < half, -1.0, 1.0).astype(F32)
    cos = jnp.cos(ang)
    sin = jnp.sin(ang) * sign[None, :]
    return jnp.tile(cos, (1, 2)), jnp.tile(sin, (1, 2))


def _nbr_bias_kernel(rpb_ref, o_ref, *, patterns):
    n_dr = 2 * WIN_R - 1
    n_dc = 2 * WIN_C - 1
    base = pl.program_id(0) * (n_dr * n_dc)
    qc = lax.broadcasted_iota(jnp.int32, (GRID_W, LANES), 0)
    lane = lax.broadcasted_iota(jnp.int32, (GRID_W, LANES), 1)
    kc = lane % GRID_W
    dc = jnp.clip(kc - qc, -(WIN_C - 1), WIN_C - 1) + WIN_C - 1
    cs = jnp.clip(qc - WIN_C // 2, 0, GRID_W - WIN_C)
    col_ok = (kc >= cs) & (kc < cs + WIN_C)
    masked = jnp.full((GRID_W, LANES), NEG, F32)
    blocks = []
    for a in range(n_dr):
        t = masked
        for d in range(n_dc):
            t = jnp.where(col_ok & (dc == d), rpb_ref[base + a * n_dc + d], t)
        blocks.append(t)
    pick = lambda a: masked if a is None else blocks[a]
    lo = lane < GRID_W
    for cls, pat in enumerate(patterns):
        for i in range(NBR_ROWS):
            for m in range(NBR_WIN // 2):
                o_ref[cls, i * GRID_W:(i + 1) * GRID_W, m * LANES:(m + 1) * LANES] = jnp.where(
                    lo, pick(pat[i][2 * m]), pick(pat[i][2 * m + 1]))


def _nbr_bias_table(rpb, rows):
    n_heads = rpb.shape[0]
    patterns = tuple(_nbr_pattern(r0, rows) for r0 in (0, NBR_ROWS, rows - NBR_ROWS))
    mq = NBR_ROWS * GRID_W
    nk = NBR_WIN * GRID_W
    return pl.pallas_call(
        functools.partial(_nbr_bias_kernel, patterns=patterns),
        grid=(n_heads,),
        in_specs=[pl.BlockSpec(memory_space=pltpu.SMEM)],
        out_specs=pl.BlockSpec((len(patterns), None, mq, nk), lambda h: (0, h, 0, 0)),
        out_shape=jax.ShapeDtypeStruct((len(patterns), n_heads, mq, nk), F32),
        compiler_params=_cparams(1),
        name="nbr_bias",
    )(rpb.astype(F32).reshape(-1))


def _block_diag_mean():
    r = jnp.arange(NORM_CHUNK) // HEAD_DIM
    return jnp.where(r[:, None] == r[None, :], 1.0 / HEAD_DIM, 0.0).astype(BF16)


def kernel(x_prompt, x_sample, cache_k_a, cache_v_a, cache_k_b, cache_v_b, c, c_ctx,
           norm_attn_w, norm_ffn_w, w_ada, b_ada,
           w_qkv_a, q_norm_a, k_norm_a, sink_a, w_o_a,
           w_qkv_b, q_norm_b, k_norm_b, rpb_b, w_o_b,
           w_up, conv_w, conv_b, w_down):
    depth = w_ada.shape[0]
    batch, seq, _ = x_prompt.shape
    dec_batch, dec_seq, _ = x_sample.shape
    nq = N_HEADS * HEAD_DIM

    cond = jnp.concatenate(
        [c_ctx[None, :], c, jnp.zeros((N_MOD_ROWS - 1 - dec_batch, D_MODEL), F32)], axis=0)
    mod = _ada_call(cond, w_ada, b_ada[:, None, :])
    mod4 = mod.reshape(depth, N_MOD_ROWS, 1, 6 * D_MODEL)

    bd = _block_diag_mean()
    rope_tabs = _rope_tables(dec_seq)
    tile_w = lambda w, n: jnp.tile(w, n)[None, :]

    xp = x_prompt.reshape(1, batch * seq, D_MODEL)
    xs = x_sample
    new_k_a, new_v_a, new_k_b, new_v_b = [], [], [], []
    for i in range(depth):
        j = i // 2
        nw_attn = norm_attn_w[i][None, :]
        nw_ffn = norm_ffn_w[i][None, :]
        if i % 2 == 0:
            nkv = N_KV_A * HEAD_DIM
            w_qkv = w_qkv_a[j].astype(BF16)
            w_o = w_o_a[j].astype(BF16)
            qn, kn = tile_w(q_norm_a[j], N_HEADS), tile_w(k_norm_a[j], N_KV_A)
            q, k, v, kd, vd = _qkv_call(xp, mod4, i, 0, nw_attn, w_qkv, qn, kn, bd, None,
                                        nkv=nkv, dup=True, f32_kv=True, tm=512)
            new_k_a.append(k.reshape(batch, seq, N_KV_A, HEAD_DIM))
            new_v_a.append(v.reshape(batch, seq, N_KV_A, HEAD_DIM))
            shp = (batch, seq, -1)
            op = _ctx_attn_call(q.reshape(shp), kd.reshape(shp), vd.reshape(shp), sink_a[j])
            q, kd, vd = _qkv_call(xs, mod4, i, 1, nw_attn, w_qkv, qn, kn, bd, rope_tabs,
                                  nkv=nkv, dup=True, f32_kv=False, tm=512)
            o_s = _win_attn_call(q, kd, vd,
                                 cache_k_a[:, j].reshape(dec_batch, -1, nkv),
                                 cache_v_a[:, j].reshape(dec_batch, -1, nkv), sink_a[j])
        else:
            nkv = nq
            w_qkv = w_qkv_b[j].astype(BF16)
            w_o = w_o_b[j].astype(BF16)
            qn, kn = tile_w(q_norm_b[j], N_HEADS), tile_w(k_norm_b[j], N_HEADS)
            q, k, v = _qkv_call(xp, mod4, i, 0, nw_attn, w_qkv, qn, kn, bd, None,
                                nkv=nkv, dup=False, f32_kv=True, tm=512)
            new_k_b.append(k.reshape(batch, seq, N_HEADS, HEAD_DIM))
            new_v_b.append(v.reshape(batch, seq, N_HEADS, HEAD_DIM))
            shp = (batch, seq, -1)
            op = _ctx_attn_call(q.reshape(shp), k.reshape(shp), v.reshape(shp), None)
            q, k, v = _qkv_call(xs, mod4, i, 1, nw_attn, w_qkv, qn, kn, bd, None,
                                nkv=nkv, dup=False, f32_kv=False, tm=512)
            o_s = _nbr_attn_call(q, k, v, _nbr_bias_table(rpb_b[j], dec_seq // GRID_W),
                                 cache_k_b[:, j].reshape(dec_batch, -1, nq),
                                 cache_v_b[:, j].reshape(dec_batch, -1, nq))
        blk_w = (w_o, nw_ffn, w_up[i].astype(BF16), conv_w[i], conv_b[i][None, :],
                 w_down[i].astype(BF16))
        xp = _block_ffn_call(xp, op.reshape(xp.shape), mod4, i, 0, *blk_w, seq_len=seq, tm=seq)
        xs = _block_ffn_call(xs, o_s, mod4, i, 1, *blk_w, seq_len=dec_seq, tm=512)
    return (xp.reshape(batch, seq, D_MODEL), xs,
            jnp.stack(new_k_a, axis=1), jnp.stack(new_v_a, axis=1),
            jnp.stack(new_k_b, axis=1), jnp.stack(new_v_b, axis=1))
```

```python
import functools

import jax
import jax.numpy as jnp
from jax import lax
from jax.experimental import pallas as pl
from jax.experimental.pallas import tpu as pltpu

D_MODEL = 1024
HEAD_DIM = 64
N_HEADS = 16
N_KV_A = 4
GRID_W = 64
WINDOW = 128
WIN_R = 8
WIN_C = 16
D_FF = 2816
ROPE_BASE = 10000.0
EPS = 1e-6
SCALE = HEAD_DIM ** -0.5

LANES = 128
HALO = 16
NBR_ROWS = 4
NBR_WIN = NBR_ROWS + WIN_R
NORM_CHUNK = 256
FF_CHUNK = 256
FF_SLOTS = 3
NEG = -1e30
N_MOD_ROWS = 8

RESIDENT = pl.Buffered(1)

F32 = jnp.float32
BF16 = jnp.bfloat16
VMEM_LIMIT = 56 * 1024 * 1024


def _cparams(n_axes):
    return pltpu.CompilerParams(
        dimension_semantics=("arbitrary",) * n_axes, vmem_limit_bytes=VMEM_LIMIT)


def _silu(x):
    return x * (1.0 / (1.0 + jnp.exp(-x)))


def _rms_modulate(x, nw, shift, scale):
    ms = jnp.mean(x * x, axis=-1, keepdims=True)
    y = x * lax.rsqrt(ms + EPS) * nw
    return y * (1.0 + scale) + shift


def _ada_kernel(c_ref, w_ref, b_ref, o_ref):
    a = _silu(c_ref[...])
    o_ref[...] = jnp.dot(a, w_ref[...], precision=lax.Precision.HIGHEST,
                         preferred_element_type=F32) + b_ref[...]


def _ada_call(cond, w_ada, b_ada):
    depth = w_ada.shape[0]
    tn = 1024
    return pl.pallas_call(
        _ada_kernel,
        grid=(depth, 6 * D_MODEL // tn),
        in_specs=[
            pl.BlockSpec((N_MOD_ROWS, D_MODEL), lambda l, n: (0, 0)),
            pl.BlockSpec((None, D_MODEL, tn), lambda l, n: (l, 0, n)),
            pl.BlockSpec((None, 1, tn), lambda l, n: (l, 0, n)),
        ],
        out_specs=pl.BlockSpec((None, N_MOD_ROWS, tn), lambda l, n: (l, 0, n)),
        out_shape=jax.ShapeDtypeStruct((depth, N_MOD_ROWS, 6 * D_MODEL), F32),
        compiler_params=_cparams(2),
        name="adaln",
    )(cond, w_ada, b_ada)


def _head_rms(t, bd, w):
    outs = []
    for c in range(t.shape[1] // NORM_CHUNK):
        tc = t[:, c * NORM_CHUNK:(c + 1) * NORM_CHUNK]
        ms = jnp.dot((tc * tc).astype(BF16), bd, preferred_element_type=F32)
        outs.append(tc * lax.rsqrt(ms + EPS))
    return jnp.concatenate(outs, axis=1) * w


def _rope(t, cos, sin_signed, lo16):
    outs = []
    for c in range(t.shape[1] // LANES):
        tc = t[:, c * LANES:(c + 1) * LANES]
        partner = jnp.where(lo16, pltpu.roll(tc, LANES - 16, axis=1), pltpu.roll(tc, 16, axis=1))
        outs.append(tc * cos + partner * sin_signed)
    return jnp.concatenate(outs, axis=1)


def _dup_heads(t, lo64):
    outs = []
    for c in range(t.shape[1] // LANES):
        tc = t[:, c * LANES:(c + 1) * LANES]
        sw = pltpu.roll(tc, HEAD_DIM, axis=1)
        outs.append(jnp.where(lo64, tc, sw))
        outs.append(jnp.where(lo64, sw, tc))
    return jnp.concatenate(outs, axis=1)


def _qkv_kernel(*refs, nkv, rope, dup, f32_kv):
    x_ref, mod_ref, nw_ref, w_ref, qn_ref, kn_ref, bd_ref = refs[:7]
    pos = 7
    if rope:
        cos_ref, sin_ref = refs[pos:pos + 2]
        pos += 2
    outs = refs[pos:]
    nq = N_HEADS * HEAD_DIM
    lane = lax.broadcasted_iota(jnp.int32, (1, LANES), 1)

    h = _rms_modulate(x_ref[...], nw_ref[...], mod_ref[:, 0:D_MODEL],
                      mod_ref[:, D_MODEL:2 * D_MODEL])
    qkv = jnp.dot(h.astype(BF16), w_ref[...], preferred_element_type=F32)
    bd = bd_ref[...]
    q = _head_rms(qkv[:, :nq], bd, qn_ref[...])
    k = _head_rms(qkv[:, nq:nq + nkv], bd, kn_ref[...])
    v = qkv[:, nq + nkv:]
    if rope:
        lo16 = (lane & 16) == 0
        q = _rope(q, cos_ref[...], sin_ref[...], lo16)
        k = _rope(k, cos_ref[...], sin_ref[...], lo16)
    outs[0][...] = (q * SCALE).astype(BF16)
    o = 1
    if f32_kv:
        outs[o][...] = k
        outs[o + 1][...] = v
        o += 2
    if dup:
        lo64 = lane < HEAD_DIM
        k = _dup_heads(k, lo64)
        v = _dup_heads(v, lo64)
    if dup or not f32_kv:
        outs[o][...] = k.astype(BF16)
        outs[o + 1][...] = v.astype(BF16)


def _qkv_call(x, mod4, layer, row0, nw, w, qn, kn, bd, rope_tabs, *, nkv, dup, f32_kv, tm):
    g_n, l_n, _ = x.shape
    rope = rope_tabs is not None
    nq = N_HEADS * HEAD_DIM
    xmap = lambda g, i: (g, i, 0)
    const = lambda g, i: (0, 0)
    in_specs = [
        pl.BlockSpec((None, tm, D_MODEL), xmap),
        pl.BlockSpec((None, None, 1, 6 * D_MODEL), lambda g, i: (layer, row0 + g, 0, 0)),
        pl.BlockSpec((1, D_MODEL), const),
        pl.BlockSpec(w.shape, const),
        pl.BlockSpec((1, nq), const),
        pl.BlockSpec((1, nkv), const),
        pl.BlockSpec((NORM_CHUNK, NORM_CHUNK), const),
    ]
    args = [x, mod4, nw, w, qn, kn, bd]
    if rope:
        in_specs += [pl.BlockSpec((tm, LANES), lambda g, i: (i, 0))] * 2
        args += list(rope_tabs)
    out_shape = [jax.ShapeDtypeStruct((g_n, l_n, nq), BF16)]
    out_specs = [pl.BlockSpec((None, tm, nq), xmap)]
    if f32_kv:
        out_shape += [jax.ShapeDtypeStruct((g_n, l_n, nkv), F32)] * 2
        out_specs += [pl.BlockSpec((None, tm, nkv), xmap)] * 2
    if dup or not f32_kv:
        wkv = nkv * 2 if dup else nkv
        out_shape += [jax.ShapeDtypeStruct((g_n, l_n, wkv), BF16)] * 2
        out_specs += [pl.BlockSpec((None, tm, wkv), xmap)] * 2
    return pl.pallas_call(
        functools.partial(_qkv_kernel, nkv=nkv, rope=rope, dup=dup, f32_kv=f32_kv),
        grid=(g_n, l_n // tm),
        in_specs=in_specs,
        out_specs=out_specs,
        out_shape=out_shape,
        compiler_params=_cparams(2),
        name="qkv",
    )(*args)


def _scores(q2_list, segs):
    lo = lax.broadcasted_iota(jnp.int32, (1, LANES), 1) < HEAD_DIM
    zero = jnp.zeros((), BF16)
    stacked = []
    for q2 in q2_list:
        stacked.append(jnp.where(lo, q2, zero))
        stacked.append(jnp.where(lo, zero, q2))
    qs = jnp.concatenate(stacked, axis=0)
    scores = []
    for k2, _, bias in segs:
        s = lax.dot_general(qs, k2, (((1,), (1,)), ((), ())), preferred_element_type=F32)
        scores.append(s if bias is None else s + bias)
    return scores


def _softmax_pv(scores, segs, sinks, m_rows):
    lo = lax.broadcasted_iota(jnp.int32, (1, LANES), 1) < HEAD_DIM
    lane_tiles = lambda t: [t[:, c * LANES:(c + 1) * LANES] for c in range(t.shape[1] // LANES)]
    mx_tile = functools.reduce(jnp.maximum, [t for s in scores for t in lane_tiles(s)])
    if sinks is not None:
        sink_tile = jnp.concatenate(
            [jnp.full((m_rows, LANES), sk, F32) for sk in sinks], axis=0)
        mx_tile = jnp.maximum(mx_tile, sink_tile)
    mx = jnp.max(mx_tile, axis=-1, keepdims=True)
    den_tile = (jnp.exp(sink_tile - mx) * (1.0 / LANES) if sinks is not None
                else jnp.zeros((scores[0].shape[0], LANES), F32))
    acc = jnp.zeros((scores[0].shape[0], LANES), F32)
    for s, (_, v2, _) in zip(scores, segs):
        p = jnp.exp(s - mx)
        den_tile = functools.reduce(jnp.add, lane_tiles(p), den_tile)
        acc = acc + jnp.dot(p.astype(BF16), v2, preferred_element_type=F32)
    rows = lambda t, n: t[n * m_rows:(n + 1) * m_rows]
    out = acc / jnp.sum(den_tile, axis=-1, keepdims=True)
    return [jnp.where(lo, rows(out, 2 * i), rows(out, 2 * i + 1))
            for i in range(out.shape[0] // (2 * m_rows))]


def _run_attention(jobs, lookahead=True):
    scores = _scores(*jobs[0][:2])
    for n, (q2_list, segs, sinks, store) in enumerate(jobs):
        more = n + 1 < len(jobs)
        if lookahead and more:
            nxt = _scores(*jobs[n + 1][:2])
        store(_softmax_pv(scores, segs, sinks, q2_list[0].shape[0]))
        if more:
            scores = nxt if lookahead else _scores(*jobs[n + 1][:2])


def _pair_slices():
    return [slice(j * LANES, (j + 1) * LANES) for j in range(N_HEADS // 2)]


def _pair_store(o_ref, sl):
    def _store(outs):
        o_ref[:, sl] = outs[0].astype(o_ref.dtype)
    return _store


def _dup_f32_to_bf16(t):
    lo64 = lax.broadcasted_iota(jnp.int32, (1, LANES), 1) < HEAD_DIM
    return _dup_heads(t, lo64).astype(BF16)


def _gqa_heads(q, segs_of_group, sink_ref, o_ref, lookahead):
    pairs_per_job = 2
    def store(pairs):
        def _store(outs):
            for p, out in zip(pairs, outs):
                o_ref[:, p * LANES:(p + 1) * LANES] = out.astype(o_ref.dtype)
        return _store

    jobs = []
    for g in range(N_KV_A):
        for first in range(2 * g, 2 * g + 2, pairs_per_job):
            pairs = list(range(first, first + pairs_per_job))
            jobs.append(([q[:, p * LANES:(p + 1) * LANES] for p in pairs], segs_of_group(g),
                         [sink_ref[2 * p + i] for p in pairs for i in range(2)], store(pairs)))
    _run_attention(jobs, lookahead)


def _ctx_attn_a_kernel(sink_ref, q_ref, k_ref, v_ref, o_ref):
    q = q_ref[...]
    k = k_ref[...]
    v = v_ref[...]
    tile = lambda t, g: t[:, g * LANES:(g + 1) * LANES]
    _gqa_heads(q, lambda g: [(tile(k, g), tile(v, g), None)], sink_ref, o_ref, True)


def _ctx_attn_b_kernel(q_ref, k_ref, v_ref, o_ref):
    q = q_ref[...]
    k = k_ref[...].astype(BF16)
    v = v_ref[...].astype(BF16)
    _run_attention([
        ([q[:, sl]], [(k[:, sl], v[:, sl], None)], None, _pair_store(o_ref, sl))
        for sl in _pair_slices()])


def _ctx_attn_call(q, k, v, sink):
    b_n, l_n, nq = q.shape
    bmap = lambda b: (b, 0, 0)
    in_specs = [pl.BlockSpec((None, l_n, nq), bmap),
                pl.BlockSpec((None, l_n, k.shape[2]), bmap),
                pl.BlockSpec((None, l_n, v.shape[2]), bmap)]
    args = [q, k, v]
    kern = _ctx_attn_b_kernel
    if sink is not None:
        in_specs = [pl.BlockSpec(memory_space=pltpu.SMEM)] + in_specs
        args = [sink] + args
        kern = _ctx_attn_a_kernel
    return pl.pallas_call(
        kern,
        grid=(b_n,),
        in_specs=in_specs,
        out_specs=pl.BlockSpec((None, l_n, nq), bmap),
        out_shape=jax.ShapeDtypeStruct((b_n, l_n, nq), BF16),
        compiler_params=_cparams(1),
        name="ctx_attn",
    )(*args)


def _win_attn_kernel(sink_ref, q_ref, kp_ref, kc_ref, kn_ref, vp_ref, vc_ref, vn_ref,
                     ck_ref, cv_ref, o_ref, kctx_ref, vctx_ref, bias_ref):
    i = pl.program_id(1)
    nb = pl.num_programs(1)
    blk = q_ref.shape[0]

    @pl.when(i == 0)
    def _():
        kctx_ref[...] = _dup_f32_to_bf16(ck_ref[...])
        vctx_ref[...] = _dup_f32_to_bf16(cv_ref[...])
        a = lax.broadcasted_iota(jnp.int32, bias_ref.shape[1:], 0) % blk
        j = lax.broadcasted_iota(jnp.int32, bias_ref.shape[1:], 1)
        band = (j >= a) & (j <= a + 2 * WINDOW)
        for cls, ok in enumerate((band & (j >= blk), band, band & (j < 2 * blk))):
            bias_ref[cls] = jnp.where(ok, 0.0, NEG).astype(F32)

    q = q_ref[...]
    k_loc = jnp.concatenate([kp_ref[...], kc_ref[...], kn_ref[...]], axis=0)
    v_loc = jnp.concatenate([vp_ref[...], vc_ref[...], vn_ref[...]], axis=0)
    bias = bias_ref[jnp.where(i == 0, 0, jnp.where(i == nb - 1, 2, 1))]
    tile = lambda t, g: t[:, g * LANES:(g + 1) * LANES]
    _gqa_heads(q, lambda g: [(tile(k_loc, g), tile(v_loc, g), bias),
                             (kctx_ref[:, g * LANES:(g + 1) * LANES],
                              vctx_ref[:, g * LANES:(g + 1) * LANES], None)],
               sink_ref, o_ref, True)


def _win_attn_call(q, kd, vd, ck, cv, sink):
    b_n, l_n, nq = q.shape
    blk = WINDOW
    nb = l_n // blk
    assert nb >= 3
    wkv = kd.shape[2]
    prev = lambda b, i: (b, jnp.maximum(i - 1, 0), 0)
    cur = lambda b, i: (b, i, 0)
    nxt = lambda b, i: (b, jnp.minimum(i + 1, nb - 1), 0)
    kv_specs = [pl.BlockSpec((None, blk, wkv), m) for m in (prev, cur, nxt)]
    ctx_spec = pl.BlockSpec((None,) + ck.shape[1:], lambda b, i: (b, 0, 0))
    return pl.pallas_call(
        _win_attn_kernel,
        grid=(b_n, nb),
        in_specs=[pl.BlockSpec(memory_space=pltpu.SMEM),
                  pl.BlockSpec((None, blk, nq), cur)] + kv_specs + kv_specs + [ctx_spec, ctx_spec],
        out_specs=pl.BlockSpec((None, blk, nq), cur),
        out_shape=jax.ShapeDtypeStruct((b_n, l_n, nq), BF16),
        scratch_shapes=[pltpu.VMEM((ck.shape[1], wkv), BF16)] * 2
        + [pltpu.VMEM((3, N_HEADS // N_KV_A * blk, 3 * blk), F32)],
        compiler_params=_cparams(2),
        name="win_attn",
    )(sink, q, kd, kd, kd, vd, vd, vd, ck, cv)


def _nbr_window_start(r0, rows):
    return jnp.clip(r0 - WIN_R // 2, 0, rows - NBR_WIN)


def _nbr_pattern(r0, rows):
    ws = min(max(r0 - WIN_R // 2, 0), rows - NBR_WIN)
    pat = []
    for i in range(NBR_ROWS):
        r = r0 + i
        rs = min(max(r - WIN_R // 2, 0), rows - WIN_R)
        pat.append(tuple((ws + w - r + WIN_R - 1) if rs <= ws + w < rs + WIN_R else None
                         for w in range(NBR_WIN)))
    return tuple(pat)


def _nbr_class(rb, n_rb):
    return jnp.where(rb == 0, 0, jnp.where(rb == n_rb - 1, 2, 1))


def _nbr_attn_kernel(q_ref, k_ref, v_ref, bias_ref, ck_ref, cv_ref, o_ref, kc_ref, vc_ref):
    @pl.when(pl.program_id(1) == 0)
    def _():
        kc_ref[...] = ck_ref[...].astype(BF16)
        vc_ref[...] = cv_ref[...].astype(BF16)

    q = q_ref[...]
    _run_attention([
        ([q[:, sl]],
         [(k_ref[0, :, sl], v_ref[0, :, sl],
           jnp.concatenate([bias_ref[2 * j], bias_ref[2 * j + 1]], axis=0)),
          (kc_ref[:, sl], vc_ref[:, sl], None)],
         None, _pair_store(o_ref, sl))
        for j, sl in enumerate(_pair_slices())])


def _nbr_attn_call(q, k, v, bias_tab, ck, cv):
    b_n, l_n, nq = q.shape
    rows = l_n // GRID_W
    n_rb = rows // NBR_ROWS
    interior = _nbr_pattern(NBR_ROWS, rows)
    assert all(_nbr_pattern(rb * NBR_ROWS, rows) == interior for rb in range(1, n_rb - 1))
    mq = NBR_ROWS * GRID_W
    nk = NBR_WIN * GRID_W
    blk = lambda b, rb: (b, rb, 0)
    ctx_spec = pl.BlockSpec((None,) + ck.shape[1:], lambda b, rb: (b, 0, 0))
    kv_spec = pl.BlockSpec(
        (pl.Element(1), pl.Element(nk), pl.Element(nq)),
        lambda b, rb: (b, _nbr_window_start(rb * NBR_ROWS, rows) * GRID_W, 0))
    bias_spec = pl.BlockSpec((None, N_HEADS, mq, nk),
                             lambda b, rb: (_nbr_class(rb, n_rb), 0, 0, 0),
                             pipeline_mode=RESIDENT)
    return pl.pallas_call(
        _nbr_attn_kernel,
        grid=(b_n, n_rb),
        in_specs=[pl.BlockSpec((None, mq, nq), blk), kv_spec, kv_spec, bias_spec,
                  ctx_spec, ctx_spec],
        out_specs=pl.BlockSpec((None, mq, nq), blk),
        out_shape=jax.ShapeDtypeStruct((b_n, l_n, nq), BF16),
        scratch_shapes=[pltpu.VMEM(ck.shape[1:], BF16)] * 2,
        compiler_params=_cparams(2),
        name="nbr_attn",
    )(q, k, v, bias_tab, ck, cv)


def _block_ffn_kernel(*refs, tiles_per_seq):
    halo = tiles_per_seq > 1
    if halo:
        x_ref, xp_ref, xn_ref, o_ref, op_ref, on_ref = refs[:6]
        refs = refs[6:]
    else:
        x_ref, o_ref = refs[:2]
        refs = refs[2:]
    mod_ref, wo_ref, nw_ref, wup_ref, cw_ref, cb_ref, wdn_ref, y_ref = refs[:8]
    u_refs = refs[8:]
    tm = x_ref.shape[0]
    n_slabs = FF_CHUNK // LANES
    gate_attn = mod_ref[:, 2 * D_MODEL:3 * D_MODEL]
    shift = mod_ref[:, 3 * D_MODEL:4 * D_MODEL]
    scale = mod_ref[:, 4 * D_MODEL:5 * D_MODEL]
    gate_ffn = mod_ref[:, 5 * D_MODEL:6 * D_MODEL]
    if halo:
        i = pl.program_id(1) % tiles_per_seq
        x_ext = jnp.concatenate([xp_ref[...], x_ref[...], xn_ref[...]], axis=0)
        o_ext = jnp.concatenate([op_ref[...], o_ref[...], on_ref[...]], axis=0)
        x1_ext = x_ext + gate_attn * jnp.dot(o_ext, wo_ref[...], preferred_element_type=F32)
        x1 = x1_ext[HALO:HALO + tm]
        row = lax.broadcasted_iota(jnp.int32, (tm + 2 * HALO, 1), 0)
        inside = ((row >= HALO) | (i > 0)) & ((row < HALO + tm) | (i < tiles_per_seq - 1))
        h = jnp.where(inside, _rms_modulate(x1_ext, nw_ref[...], shift, scale), 0.0).astype(BF16)
        u_rows = slice(None)
    else:
        x1 = x_ref[...] + gate_attn * jnp.dot(o_ref[...], wo_ref[...],
                                              preferred_element_type=F32)
        h = _rms_modulate(x1, nw_ref[...], shift, scale).astype(BF16)
        u_rows = pl.ds(HALO, tm)
        pad = jnp.zeros((8, LANES), F32)
        for u_ref in u_refs:
            for s in range(n_slabs):
                u_ref[s, pl.ds(HALO - 8, 8), :] = pad
                u_ref[s, pl.ds(HALO + tm, 8), :] = pad

    def conv(u_ref, col):
        outs = []
        for s in range(n_slabs):
            lanes = slice(col + s * LANES, col + (s + 1) * LANES)
            cw = cw_ref[:, lanes]
            out = cb_ref[:, lanes]
            for o in range(3):
                out = out + u_ref[s, pl.ds(HALO - 1 + o, tm), :] * cw[o:o + 1, :]
            outs.append(out)
        return jnp.concatenate(outs, axis=1)

    def up_one(u_ref, col):
        u = jnp.dot(h, wup_ref[:, col:col + FF_CHUNK], preferred_element_type=F32)
        for s in range(n_slabs):
            u_ref[s, u_rows, :] = u[:, s * LANES:(s + 1) * LANES]

    def up(c):
        up_one(u_refs[2 * (c % FF_SLOTS)], c * FF_CHUNK)
        up_one(u_refs[2 * (c % FF_SLOTS) + 1], D_FF + c * FF_CHUNK)

    n_chunks = D_FF // FF_CHUNK
    acc = jnp.zeros((tm, D_MODEL), F32)
    up(0)
    for c in range(n_chunks):
        if c + 1 < n_chunks:
            up(c + 1)
        ug_ref, uv_ref = u_refs[2 * (c % FF_SLOTS)], u_refs[2 * (c % FF_SLOTS) + 1]
        gc = c * FF_CHUNK
        act = (_silu(conv(ug_ref, gc)) * conv(uv_ref, D_FF + gc)).astype(BF16)
        acc = acc + jnp.dot(act, wdn_ref[gc:gc + FF_CHUNK, :], preferred_element_type=F32)
    y_ref[...] = x1 + gate_ffn * acc


def _block_ffn_call(x, o, mod4, layer, row0, w_o, nw, w_up, conv_w, conv_b, w_down, *,
                    seq_len, tm):
    g_n, l_n, _ = x.shape
    tiles_per_seq = seq_len // tm
    nh = l_n // HALO
    hb = tm // HALO
    const = lambda g, i: (0, 0)
    tile = lambda g, i: (g, i, 0)
    prev = lambda g, i: (g, jnp.maximum(i * hb - 1, 0), 0)
    nxt = lambda g, i: (g, jnp.minimum((i + 1) * hb, nh - 1), 0)
    tile_specs = [pl.BlockSpec((None, tm, D_MODEL), tile)]
    x_args, o_args = [x], [o]
    if tiles_per_seq > 1:
        tile_specs += [pl.BlockSpec((None, HALO, D_MODEL), prev),
                       pl.BlockSpec((None, HALO, D_MODEL), nxt)]
        x_args, o_args = [x] * 3, [o] * 3
    return pl.pallas_call(
        functools.partial(_block_ffn_kernel, tiles_per_seq=tiles_per_seq),
        grid=(g_n, l_n // tm),
        in_specs=tile_specs + tile_specs + [
            pl.BlockSpec((None, None, 1, 6 * D_MODEL), lambda g, i: (layer, row0 + g, 0, 0)),
            pl.BlockSpec(w_o.shape, const, pipeline_mode=RESIDENT),
            pl.BlockSpec((1, D_MODEL), const),
            pl.BlockSpec(w_up.shape, const, pipeline_mode=RESIDENT),
            pl.BlockSpec(conv_w.shape, const),
            pl.BlockSpec(conv_b.shape, const),
            pl.BlockSpec(w_down.shape, const, pipeline_mode=RESIDENT),
        ],
        out_specs=pl.BlockSpec((None, tm, D_MODEL), tile),
        out_shape=jax.ShapeDtypeStruct(x.shape, F32),
        scratch_shapes=[pltpu.VMEM((FF_CHUNK // LANES, tm + 2 * HALO, LANES), F32)]
        * (2 * FF_SLOTS),
        compiler_params=_cparams(2),
        name="proj_conv_ffn",
    )(*x_args, *o_args, mod4, w_o, nw, w_up, conv_w, conv_b, w_down)


def _rope_tables(l_n):
    half = HEAD_DIM // 4
    freqs = ROPE_BASE ** (-jnp.arange(half, dtype=F32) / half)
    t = jnp.arange(l_n)
    lane = jnp.arange(HEAD_DIM)
    pos = jnp.where(lane[None, :] < HEAD_DIM // 2, (t // GRID_W)[:, None], (t % GRID_W)[:, None])
    ang = pos.astype(F32) * freqs[lane % half][None, :]
    sign = jnp.where((lane % (2 * half)) < half, -1.0, 1.0).astype(F32)
    cos = jnp.cos(ang)
    sin = jnp.sin(ang) * sign[None, :]
    return jnp.tile(cos, (1, 2)), jnp.tile(sin, (1, 2))


def _nbr_bias_kernel(rpb_ref, o_ref, *, patterns):
    n_dr = 2 * WIN_R - 1
    n_dc = 2 * WIN_C - 1
    base = pl.program_id(0) * (n_dr * n_dc)
    qc = lax.broadcasted_iota(jnp.int32, (GRID_W, LANES), 0)
    lane = lax.broadcasted_iota(jnp.int32, (GRID_W, LANES), 1)
    kc = lane % GRID_W
    dc = jnp.clip(kc - qc, -(WIN_C - 1), WIN_C - 1) + WIN_C - 1
    cs = jnp.clip(qc - WIN_C // 2, 0, GRID_W - WIN_C)
    col_ok = (kc >= cs) & (kc < cs + WIN_C)
    masked = jnp.full((GRID_W, LANES), NEG, F32)
    blocks = []
    for a in range(n_dr):
        t = masked
        for d in range(n_dc):
            t = jnp.where(col_ok & (dc == d), rpb_ref[base + a * n_dc + d], t)
        blocks.append(t)
    pick = lambda a: masked if a is None else blocks[a]
    lo = lane < GRID_W
    for cls, pat in enumerate(patterns):
        for i in range(NBR_ROWS):
            for m in range(NBR_WIN // 2):
                o_ref[cls, i * GRID_W:(i + 1) * GRID_W, m * LANES:(m + 1) * LANES] = jnp.where(
                    lo, pick(pat[i][2 * m]), pick(pat[i][2 * m + 1]))


def _nbr_bias_table(rpb, rows):
    n_heads = rpb.shape[0]
    patterns = tuple(_nbr_pattern(r0, rows) for r0 in (0, NBR_ROWS, rows - NBR_ROWS))
    mq = NBR_ROWS * GRID_W
    nk = NBR_WIN * GRID_W
    return pl.pallas_call(
        functools.partial(_nbr_bias_kernel, patterns=patterns),
        grid=(n_heads,),
        in_specs=[pl.BlockSpec(memory_space=pltpu.SMEM)],
        out_specs=pl.BlockSpec((len(patterns), None, mq, nk), lambda h: (0, h, 0, 0)),
        out_shape=jax.ShapeDtypeStruct((len(patterns), n_heads, mq, nk), F32),
        compiler_params=_cparams(1),
        name="nbr_bias",
    )(rpb.astype(F32).reshape(-1))


def _block_diag_mean():
    r = jnp.arange(NORM_CHUNK) // HEAD_DIM
    return jnp.where(r[:, None] == r[None, :], 1.0 / HEAD_DIM, 0.0).astype(BF16)


def kernel(x_prompt, x_sample, cache_k_a, cache_v_a, cache_k_b, cache_v_b, c, c_ctx,
           norm_attn_w, norm_ffn_w, w_ada, b_ada,
           w_qkv_a, q_norm_a, k_norm_a, sink_a, w_o_a,
           w_qkv_b, q_norm_b, k_norm_b, rpb_b, w_o_b,
           w_up, conv_w, conv_b, w_down):
    depth = w_ada.shape[0]
    batch, seq, _ = x_prompt.shape
    dec_batch, dec_seq, _ = x_sample.shape
    nq = N_HEADS * HEAD_DIM

    cond = jnp.concatenate(
        [c_ctx[None, :], c, jnp.zeros((N_MOD_ROWS - 1 - dec_batch, D_MODEL), F32)], axis=0)
    mod = _ada_call(cond, w_ada, b_ada[:, None, :])
    mod4 = mod.reshape(depth, N_MOD_ROWS, 1, 6 * D_MODEL)

    bd = _block_diag_mean()
    rope_tabs = _rope_tables(dec_seq)
    tile_w = lambda w, n: jnp.tile(w, n)[None, :]

    xp = x_prompt.reshape(1, batch * seq, D_MODEL)
    xs = x_sample
    new_k_a, new_v_a, new_k_b, new_v_b = [], [], [], []
    for i in range(depth):
        j = i // 2
        nw_attn = norm_attn_w[i][None, :]
        nw_ffn = norm_ffn_w[i][None, :]
        if i % 2 == 0:
            nkv = N_KV_A * HEAD_DIM
            w_qkv = w_qkv_a[j].astype(BF16)
            w_o = w_o_a[j].astype(BF16)
            qn, kn = tile_w(q_norm_a[j], N_HEADS), tile_w(k_norm_a[j], N_KV_A)
            q, k, v, kd, vd = _qkv_call(xp, mod4, i, 0, nw_attn, w_qkv, qn, kn, bd, None,
                                        nkv=nkv, dup=True, f32_kv=True, tm=512)
            new_k_a.append(k.reshape(batch, seq, N_KV_A, HEAD_DIM))
            new_v_a.append(v.reshape(batch, seq, N_KV_A, HEAD_DIM))
            shp = (batch, seq, -1)
            op = _ctx_attn_call(q.reshape(shp), kd.reshape(shp), vd.reshape(shp), sink_a[j])
            q, kd, vd = _qkv_call(xs, mod4, i, 1, nw_attn, w_qkv, qn, kn, bd, rope_tabs,
                                  nkv=nkv, dup=True, f32_kv=False, tm=512)
            o_s = _win_attn_call(q, kd, vd,
                                 cache_k_a[:, j].reshape(dec_batch, -1, nkv),
                                 cache_v_a[:, j].reshape(dec_batch, -1, nkv), sink_a[j])
        else:
            nkv = nq
            w_qkv = w_qkv_b[j].astype(BF16)
            w_o = w_o_b[j].astype(BF16)
            qn, kn = tile_w(q_norm_b[j], N_HEADS), tile_w(k_norm_b[j], N_HEADS)
            q, k, v = _qkv_call(xp, mod4, i, 0, nw_attn, w_qkv, qn, kn, bd, None,
                                nkv=nkv, dup=False, f32_kv=True, tm=512)
            new_k_b.append(k.reshape(batch, seq, N_HEADS, HEAD_DIM))
            new_v_b.append(v.reshape(batch, seq, N_HEADS, HEAD_DIM))
            shp = (batch, seq, -1)
            op = _ctx_attn_call(q.reshape(shp), k.reshape(shp), v.reshape(shp), None)
            q, k, v = _qkv_call(xs, mod4, i, 1, nw_attn, w_qkv, qn, kn, bd, None,
                                nkv=nkv, dup=False, f32_kv=False, tm=512)
            o_s = _nbr_attn_call(q, k, v, _nbr_bias_table(rpb_b[j], dec_seq // GRID_W),
                                 cache_k_b[:, j].reshape(dec_batch, -1, nq),
                                 cache_v_b[:, j].reshape(dec_batch, -1, nq))
        blk_w = (w_o, nw_ffn, w_up[i].astype(BF16), conv_w[i], conv_b[i][None, :],
                 w_down[i].astype(BF16))
        xp = _block_ffn_call(xp, op.reshape(xp.shape), mod4, i, 0, *blk_w, seq_len=seq, tm=seq)
        xs = _block_ffn_call(xs, o_s, mod4, i, 1, *blk_w, seq_len=dec_seq, tm=512)
    return (xp.reshape(batch, seq, D_MODEL), xs,
            jnp.stack(new_k_a, axis=1), jnp.stack(new_v_a, axis=1),
            jnp.stack(new_k_b, axis=1), jnp.stack(new_v_b, axis=1))
```

```python
import functools

import jax
import jax.numpy as jnp
from jax import lax
from jax.experimental import pallas as pl
from jax.experimental.pallas import tpu as pltpu

D_MODEL = 1024
HEAD_DIM = 64
N_HEADS = 16
N_KV_A = 4
GRID_W = 64
WINDOW = 128
WIN_R = 8
WIN_C = 16
D_FF = 2816
ROPE_BASE = 10000.0
EPS = 1e-6
SCALE = HEAD_DIM ** -0.5

LANES = 128
HALO = 16
NBR_ROWS = 4
NBR_WIN = NBR_ROWS + WIN_R
NORM_CHUNK = 256
FF_CHUNK = 256
FF_SLOTS = 3
NEG = -1e30
N_MOD_ROWS = 8

RESIDENT = pl.Buffered(1)

F32 = jnp.float32
BF16 = jnp.bfloat16
VMEM_LIMIT = 56 * 1024 * 1024


def _cparams(n_axes):
    return pltpu.CompilerParams(
        dimension_semantics=("arbitrary",) * n_axes, vmem_limit_bytes=VMEM_LIMIT)


def _silu(x):
    return x * (1.0 / (1.0 + jnp.exp(-x)))


def _rms_modulate(x, nw, shift, scale):
    ms = jnp.mean(x * x, axis=-1, keepdims=True)
    y = x * lax.rsqrt(ms + EPS) * nw
    return y * (1.0 + scale) + shift


def _ada_kernel(ct_ref, w_ref, b_ref, o_ref, *, n_cond):
    a = _silu(ct_ref[...])
    w = w_ref[...]
    rows = [jnp.sum(a[:, r:r + 1] * w, axis=0, keepdims=True) for r in range(n_cond)]
    rows.append(jnp.zeros((N_MOD_ROWS - n_cond, w.shape[1]), F32))
    o_ref[...] = jnp.concatenate(rows, axis=0) + b_ref[...]


def _ada_call(cond_t, n_cond, w_ada, b_ada):
    depth = w_ada.shape[0]
    tn = 1024
    return pl.pallas_call(
        functools.partial(_ada_kernel, n_cond=n_cond),
        grid=(depth, 6 * D_MODEL // tn),
        in_specs=[
            pl.BlockSpec((D_MODEL, N_MOD_ROWS), lambda l, n: (0, 0)),
            pl.BlockSpec((None, D_MODEL, tn), lambda l, n: (l, 0, n)),
            pl.BlockSpec((None, 1, tn), lambda l, n: (l, 0, n)),
        ],
        out_specs=pl.BlockSpec((None, N_MOD_ROWS, tn), lambda l, n: (l, 0, n)),
        out_shape=jax.ShapeDtypeStruct((depth, N_MOD_ROWS, 6 * D_MODEL), F32),
        compiler_params=_cparams(2),
        name="adaln",
    )(cond_t, w_ada, b_ada)


def _head_rms(t, bd, w):
    outs = []
    for c in range(t.shape[1] // NORM_CHUNK):
        tc = t[:, c * NORM_CHUNK:(c + 1) * NORM_CHUNK]
        ms = jnp.dot((tc * tc).astype(BF16), bd, preferred_element_type=F32)
        outs.append(tc * lax.rsqrt(ms + EPS))
    return jnp.concatenate(outs, axis=1) * w


def _rope(t, cos, sin_signed, lo16):
    outs = []
    for c in range(t.shape[1] // LANES):
        tc = t[:, c * LANES:(c + 1) * LANES]
        partner = jnp.where(lo16, pltpu.roll(tc, LANES - 16, axis=1), pltpu.roll(tc, 16, axis=1))
        outs.append(tc * cos + partner * sin_signed)
    return jnp.concatenate(outs, axis=1)


def _dup_heads(t, lo64):
    outs = []
    for c in range(t.shape[1] // LANES):
        tc = t[:, c * LANES:(c + 1) * LANES]
        sw = pltpu.roll(tc, HEAD_DIM, axis=1)
        outs.append(jnp.where(lo64, tc, sw))
        outs.append(jnp.where(lo64, sw, tc))
    return jnp.concatenate(outs, axis=1)


def _qkv_kernel(*refs, nkv, rope, dup, f32_kv):
    x_ref, mod_ref, nw_ref, w_ref, qn_ref, kn_ref, bd_ref = refs[:7]
    pos = 7
    if rope:
        cos_ref, sin_ref = refs[pos:pos + 2]
        pos += 2
    outs = refs[pos:]
    nq = N_HEADS * HEAD_DIM
    lane = lax.broadcasted_iota(jnp.int32, (1, LANES), 1)

    h = _rms_modulate(x_ref[...], nw_ref[...], mod_ref[:, 0:D_MODEL],
                      mod_ref[:, D_MODEL:2 * D_MODEL])
    qkv = jnp.dot(h.astype(BF16), w_ref[...], preferred_element_type=F32)
    bd = bd_ref[...]
    q = _head_rms(qkv[:, :nq], bd, qn_ref[...])
    k = _head_rms(qkv[:, nq:nq + nkv], bd, kn_ref[...])
    v = qkv[:, nq + nkv:]
    if rope:
        lo16 = (lane & 16) == 0
        q = _rope(q, cos_ref[...], sin_ref[...], lo16)
        k = _rope(k, cos_ref[...], sin_ref[...], lo16)
    outs[0][...] = (q * SCALE).astype(BF16)
    o = 1
    if f32_kv:
        outs[o][...] = k
        outs[o + 1][...] = v
        o += 2
    if dup:
        lo64 = lane < HEAD_DIM
        k = _dup_heads(k, lo64)
        v = _dup_heads(v, lo64)
    if dup or not f32_kv:
        outs[o][...] = k.astype(BF16)
        outs[o + 1][...] = v.astype(BF16)


def _qkv_call(x, mod4, layer, row0, nw, w, qn, kn, bd, rope_tabs, *, nkv, dup, f32_kv, tm):
    g_n, l_n, _ = x.shape
    rope = rope_tabs is not None
    nq = N_HEADS * HEAD_DIM
    xmap = lambda g, i: (g, i, 0)
    const = lambda g, i: (0, 0)
    in_specs = [
        pl.BlockSpec((None, tm, D_MODEL), xmap),
        pl.BlockSpec((None, None, 1, 6 * D_MODEL), lambda g, i: (layer, row0 + g, 0, 0)),
        pl.BlockSpec((1, D_MODEL), const),
        pl.BlockSpec(w.shape, const),
        pl.BlockSpec((1, nq), const),
        pl.BlockSpec((1, nkv), const),
        pl.BlockSpec((NORM_CHUNK, NORM_CHUNK), const),
    ]
    args = [x, mod4, nw, w, qn, kn, bd]
    if rope:
        in_specs += [pl.BlockSpec((tm, LANES), lambda g, i: (i, 0))] * 2
        args += list(rope_tabs)
    out_shape = [jax.ShapeDtypeStruct((g_n, l_n, nq), BF16)]
    out_specs = [pl.BlockSpec((None, tm, nq), xmap)]
    if f32_kv:
        out_shape += [jax.ShapeDtypeStruct((g_n, l_n, nkv), F32)] * 2
        out_specs += [pl.BlockSpec((None, tm, nkv), xmap)] * 2
    if dup or not f32_kv:
        wkv = nkv * 2 if dup else nkv
        out_shape += [jax.ShapeDtypeStruct((g_n, l_n, wkv), BF16)] * 2
        out_specs += [pl.BlockSpec((None, tm, wkv), xmap)] * 2
    return pl.pallas_call(
        functools.partial(_qkv_kernel, nkv=nkv, rope=rope, dup=dup, f32_kv=f32_kv),
        grid=(g_n, l_n // tm),
        in_specs=in_specs,
        out_specs=out_specs,
        out_shape=out_shape,
        compiler_params=_cparams(2),
        name="qkv",
    )(*args)


def _scores(q2_list, segs):
    lo = lax.broadcasted_iota(jnp.int32, (1, LANES), 1) < HEAD_DIM
    zero = jnp.zeros((), BF16)
    stacked = []
    for q2 in q2_list:
        stacked.append(jnp.where(lo, q2, zero))
        stacked.append(jnp.where(lo, zero, q2))
    qs = jnp.concatenate(stacked, axis=0)
    scores = []
    for k2, _, bias in segs:
        s = lax.dot_general(qs, k2, (((1,), (1,)), ((), ())), preferred_element_type=F32)
        scores.append(s if bias is None else s + bias)
    return scores


def _softmax_pv(scores, segs, sinks, m_rows):
    lo = lax.broadcasted_iota(jnp.int32, (1, LANES), 1) < HEAD_DIM
    lane_tiles = lambda t: [t[:, c * LANES:(c + 1) * LANES] for c in range(t.shape[1] // LANES)]
    mx_tile = functools.reduce(jnp.maximum, [t for s in scores for t in lane_tiles(s)])
    if sinks is not None:
        sink_tile = jnp.concatenate(
            [jnp.full((m_rows, LANES), sk, F32) for sk in sinks], axis=0)
        mx_tile = jnp.maximum(mx_tile, sink_tile)
    mx = jnp.max(mx_tile, axis=-1, keepdims=True)
    den_tile = (jnp.exp(sink_tile - mx) * (1.0 / LANES) if sinks is not None
                else jnp.zeros((scores[0].shape[0], LANES), F32))
    acc = jnp.zeros((scores[0].shape[0], LANES), F32)
    for s, (_, v2, _) in zip(scores, segs):
        p = jnp.exp(s - mx)
        den_tile = functools.reduce(jnp.add, lane_tiles(p), den_tile)
        acc = acc + jnp.dot(p.astype(BF16), v2, preferred_element_type=F32)
    rows = lambda t, n: t[n * m_rows:(n + 1) * m_rows]
    out = acc / jnp.sum(den_tile, axis=-1, keepdims=True)
    return [jnp.where(lo, rows(out, 2 * i), rows(out, 2 * i + 1))
            for i in range(out.shape[0] // (2 * m_rows))]


def _run_attention(jobs, lookahead=True):
    scores = _scores(*jobs[0][:2])
    for n, (q2_list, segs, sinks, store) in enumerate(jobs):
        more = n + 1 < len(jobs)
        if lookahead and more:
            nxt = _scores(*jobs[n + 1][:2])
        store(_softmax_pv(scores, segs, sinks, q2_list[0].shape[0]))
        if more:
            scores = nxt if lookahead else _scores(*jobs[n + 1][:2])


def _pair_slices():
    return [slice(j * LANES, (j + 1) * LANES) for j in range(N_HEADS // 2)]


def _pair_store(o_ref, sl):
    def _store(outs):
        o_ref[:, sl] = outs[0].astype(o_ref.dtype)
    return _store


def _dup_f32_to_bf16(t):
    lo64 = lax.broadcasted_iota(jnp.int32, (1, LANES), 1) < HEAD_DIM
    return _dup_heads(t, lo64).astype(BF16)


def _gqa_heads(q, segs_of_group, sink_ref, o_ref, lookahead):
    pairs_per_job = 2
    def store(pairs):
        def _store(outs):
            for p, out in zip(pairs, outs):
                o_ref[:, p * LANES:(p + 1) * LANES] = out.astype(o_ref.dtype)
        return _store

    jobs = []
    for g in range(N_KV_A):
        for first in range(2 * g, 2 * g + 2, pairs_per_job):
            pairs = list(range(first, first + pairs_per_job))
            jobs.append(([q[:, p * LANES:(p + 1) * LANES] for p in pairs], segs_of_group(g),
                         [sink_ref[2 * p + i] for p in pairs for i in range(2)], store(pairs)))
    _run_attention(jobs, lookahead)


def _ctx_attn_a_kernel(sink_ref, q_ref, k_ref, v_ref, o_ref):
    q = q_ref[...]
    k = k_ref[...]
    v = v_ref[...]
    tile = lambda t, g: t[:, g * LANES:(g + 1) * LANES]
    _gqa_heads(q, lambda g: [(tile(k, g), tile(v, g), None)], sink_ref, o_ref, True)


def _ctx_attn_b_kernel(q_ref, k_ref, v_ref, o_ref):
    q = q_ref[...]
    k = k_ref[...].astype(BF16)
    v = v_ref[...].astype(BF16)
    _run_attention([
        ([q[:, sl]], [(k[:, sl], v[:, sl], None)], None, _pair_store(o_ref, sl))
        for sl in _pair_slices()])


def _ctx_attn_call(q, k, v, sink):
    b_n, l_n, nq = q.shape
    bmap = lambda b: (b, 0, 0)
    in_specs = [pl.BlockSpec((None, l_n, nq), bmap),
                pl.BlockSpec((None, l_n, k.shape[2]), bmap),
                pl.BlockSpec((None, l_n, v.shape[2]), bmap)]
    args = [q, k, v]
    kern = _ctx_attn_b_kernel
    if sink is not None:
        in_specs = [pl.BlockSpec(memory_space=pltpu.SMEM)] + in_specs
        args = [sink] + args
        kern = _ctx_attn_a_kernel
    return pl.pallas_call(
        kern,
        grid=(b_n,),
        in_specs=in_specs,
        out_specs=pl.BlockSpec((None, l_n, nq), bmap),
        out_shape=jax.ShapeDtypeStruct((b_n, l_n, nq), BF16),
        compiler_params=_cparams(1),
        name="ctx_attn",
    )(*args)


def _win_attn_kernel(sink_ref, q_ref, kp_ref, kc_ref, kn_ref, vp_ref, vc_ref, vn_ref,
                     ck_ref, cv_ref, o_ref, kctx_ref, vctx_ref, bias_ref):
    i = pl.program_id(1)
    nb = pl.num_programs(1)
    blk = q_ref.shape[0]

    @pl.when(i == 0)
    def _():
        kctx_ref[...] = _dup_f32_to_bf16(ck_ref[...])
        vctx_ref[...] = _dup_f32_to_bf16(cv_ref[...])
        a = lax.broadcasted_iota(jnp.int32, bias_ref.shape[1:], 0) % blk
        j = lax.broadcasted_iota(jnp.int32, bias_ref.shape[1:], 1)
        band = (j >= a) & (j <= a + 2 * WINDOW)
        for cls, ok in enumerate((band & (j >= blk), band, band & (j < 2 * blk))):
            bias_ref[cls] = jnp.where(ok, 0.0, NEG).astype(F32)

    q = q_ref[...]
    k_loc = jnp.concatenate([kp_ref[...], kc_ref[...], kn_ref[...]], axis=0)
    v_loc = jnp.concatenate([vp_ref[...], vc_ref[...], vn_ref[...]], axis=0)
    bias = bias_ref[jnp.where(i == 0, 0, jnp.where(i == nb - 1, 2, 1))]
    tile = lambda t, g: t[:, g * LANES:(g + 1) * LANES]
    _gqa_heads(q, lambda g: [(tile(k_loc, g), tile(v_loc, g), bias),
                             (kctx_ref[:, g * LANES:(g + 1) * LANES],
                              vctx_ref[:, g * LANES:(g + 1) * LANES], None)],
               sink_ref, o_ref, True)


def _win_attn_call(q, kd, vd, ck, cv, sink):
    b_n, l_n, nq = q.shape
    blk = WINDOW
    nb = l_n // blk
    assert nb >= 3
    wkv = kd.shape[2]
    prev = lambda b, i: (b, jnp.maximum(i - 1, 0), 0)
    cur = lambda b, i: (b, i, 0)
    nxt = lambda b, i: (b, jnp.minimum(i + 1, nb - 1), 0)
    kv_specs = [pl.BlockSpec((None, blk, wkv), m) for m in (prev, cur, nxt)]
    ctx_spec = pl.BlockSpec((None,) + ck.shape[1:], lambda b, i: (b, 0, 0))
    return pl.pallas_call(
        _win_attn_kernel,
        grid=(b_n, nb),
        in_specs=[pl.BlockSpec(memory_space=pltpu.SMEM),
                  pl.BlockSpec((None, blk, nq), cur)] + kv_specs + kv_specs + [ctx_spec, ctx_spec],
        out_specs=pl.BlockSpec((None, blk, nq), cur),
        out_shape=jax.ShapeDtypeStruct((b_n, l_n, nq), BF16),
        scratch_shapes=[pltpu.VMEM((ck.shape[1], wkv), BF16)] * 2
        + [pltpu.VMEM((3, N_HEADS // N_KV_A * blk, 3 * blk), F32)],
        compiler_params=_cparams(2),
        name="win_attn",
    )(sink, q, kd, kd, kd, vd, vd, vd, ck, cv)


def _nbr_window_start(r0, rows):
    return jnp.clip(r0 - WIN_R // 2, 0, rows - NBR_WIN)


def _nbr_pattern(r0, rows):
    ws = min(max(r0 - WIN_R // 2, 0), rows - NBR_WIN)
    pat = []
    for i in range(NBR_ROWS):
        r = r0 + i
        rs = min(max(r - WIN_R // 2, 0), rows - WIN_R)
        pat.append(tuple((ws + w - r + WIN_R - 1) if rs <= ws + w < rs + WIN_R else None
                         for w in range(NBR_WIN)))
    return tuple(pat)


def _nbr_class(rb, n_rb):
    return jnp.where(rb == 0, 0, jnp.where(rb == n_rb - 1, 2, 1))


def _nbr_attn_kernel(q_ref, k_ref, v_ref, bias_ref, ck_ref, cv_ref, o_ref, kc_ref, vc_ref):
    @pl.when(pl.program_id(1) == 0)
    def _():
        kc_ref[...] = ck_ref[...].astype(BF16)
        vc_ref[...] = cv_ref[...].astype(BF16)

    q = q_ref[...]
    _run_attention([
        ([q[:, sl]],
         [(k_ref[0, :, sl], v_ref[0, :, sl],
           jnp.concatenate([bias_ref[2 * j], bias_ref[2 * j + 1]], axis=0)),
          (kc_ref[:, sl], vc_ref[:, sl], None)],
         None, _pair_store(o_ref, sl))
        for j, sl in enumerate(_pair_slices())])


def _nbr_attn_call(q, k, v, bias_tab, ck, cv):
    b_n, l_n, nq = q.shape
    rows = l_n // GRID_W
    n_rb = rows // NBR_ROWS
    interior = _nbr_pattern(NBR_ROWS, rows)
    assert all(_nbr_pattern(rb * NBR_ROWS, rows) == interior for rb in range(1, n_rb - 1))
    mq = NBR_ROWS * GRID_W
    nk = NBR_WIN * GRID_W
    blk = lambda b, rb: (b, rb, 0)
    ctx_spec = pl.BlockSpec((None,) + ck.shape[1:], lambda b, rb: (b, 0, 0))
    kv_spec = pl.BlockSpec(
        (pl.Element(1), pl.Element(nk), pl.Element(nq)),
        lambda b, rb: (b, _nbr_window_start(rb * NBR_ROWS, rows) * GRID_W, 0))
    bias_spec = pl.BlockSpec((None, N_HEADS, mq, nk),
                             lambda b, rb: (_nbr_class(rb, n_rb), 0, 0, 0),
                             pipeline_mode=RESIDENT)
    return pl.pallas_call(
        _nbr_attn_kernel,
        grid=(b_n, n_rb),
        in_specs=[pl.BlockSpec((None, mq, nq), blk), kv_spec, kv_spec, bias_spec,
                  ctx_spec, ctx_spec],
        out_specs=pl.BlockSpec((None, mq, nq), blk),
        out_shape=jax.ShapeDtypeStruct((b_n, l_n, nq), BF16),
        scratch_shapes=[pltpu.VMEM(ck.shape[1:], BF16)] * 2,
        compiler_params=_cparams(2),
        name="nbr_attn",
    )(q, k, v, bias_tab, ck, cv)


def _block_ffn_kernel(*refs, tiles_per_seq):
    halo = tiles_per_seq > 1
    if halo:
        x_ref, xp_ref, xn_ref, o_ref, op_ref, on_ref = refs[:6]
        refs = refs[6:]
    else:
        x_ref, o_ref = refs[:2]
        refs = refs[2:]
    mod_ref, wo_ref, nw_ref, wup_ref, cw_ref, cb_ref, wdn_ref, y_ref = refs[:8]
    u_refs = refs[8:]
    tm = x_ref.shape[0]
    n_slabs = FF_CHUNK // LANES
    gate_attn = mod_ref[:, 2 * D_MODEL:3 * D_MODEL]
    shift = mod_ref[:, 3 * D_MODEL:4 * D_MODEL]
    scale = mod_ref[:, 4 * D_MODEL:5 * D_MODEL]
    gate_ffn = mod_ref[:, 5 * D_MODEL:6 * D_MODEL]
    if halo:
        i = pl.program_id(1) % tiles_per_seq
        x_ext = jnp.concatenate([xp_ref[...], x_ref[...], xn_ref[...]], axis=0)
        o_ext = jnp.concatenate([op_ref[...], o_ref[...], on_ref[...]], axis=0)
        x1_ext = x_ext + gate_attn * jnp.dot(o_ext, wo_ref[...], preferred_element_type=F32)
        x1 = x1_ext[HALO:HALO + tm]
        row = lax.broadcasted_iota(jnp.int32, (tm + 2 * HALO, 1), 0)
        inside = ((row >= HALO) | (i > 0)) & ((row < HALO + tm) | (i < tiles_per_seq - 1))
        h = jnp.where(inside, _rms_modulate(x1_ext, nw_ref[...], shift, scale), 0.0).astype(BF16)
        u_rows = slice(None)
    else:
        x1 = x_ref[...] + gate_attn * jnp.dot(o_ref[...], wo_ref[...],
                                              preferred_element_type=F32)
        h = _rms_modulate(x1, nw_ref[...], shift, scale).astype(BF16)
        u_rows = pl.ds(HALO, tm)
        pad = jnp.zeros((8, LANES), F32)
        for u_ref in u_refs:
            for s in range(n_slabs):
                u_ref[s, pl.ds(HALO - 8, 8), :] = pad
                u_ref[s, pl.ds(HALO + tm, 8), :] = pad

    def conv(u_ref, col):
        outs = []
        for s in range(n_slabs):
            lanes = slice(col + s * LANES, col + (s + 1) * LANES)
            cw = cw_ref[:, lanes]
            out = cb_ref[:, lanes]
            for o in range(3):
                out = out + u_ref[s, pl.ds(HALO - 1 + o, tm), :] * cw[o:o + 1, :]
            outs.append(out)
        return jnp.concatenate(outs, axis=1)

    def up_one(u_ref, col):
        u = jnp.dot(h, wup_ref[:, col:col + FF_CHUNK], preferred_element_type=F32)
        for s in range(n_slabs):
            u_ref[s, u_rows, :] = u[:, s * LANES:(s + 1) * LANES]

    def up(c):
        up_one(u_refs[2 * (c % FF_SLOTS)], c * FF_CHUNK)
        up_one(u_refs[2 * (c % FF_SLOTS) + 1], D_FF + c * FF_CHUNK)

    n_chunks = D_FF // FF_CHUNK
    acc = jnp.zeros((tm, D_MODEL), F32)
    up(0)
    for c in range(n_chunks):
        if c + 1 < n_chunks:
            up(c + 1)
        ug_ref, uv_ref = u_refs[2 * (c % FF_SLOTS)], u_refs[2 * (c % FF_SLOTS) + 1]
        gc = c * FF_CHUNK
        act = (_silu(conv(ug_ref, gc)) * conv(uv_ref, D_FF + gc)).astype(BF16)
        acc = acc + jnp.dot(act, wdn_ref[gc:gc + FF_CHUNK, :], preferred_element_type=F32)
    y_ref[...] = x1 + gate_ffn * acc


def _block_ffn_call(x, o, mod4, layer, row0, w_o, nw, w_up, conv_w, conv_b, w_down, *,
                    seq_len, tm):
    g_n, l_n, _ = x.shape
    tiles_per_seq = seq_len // tm
    nh = l_n // HALO
    hb = tm // HALO
    const = lambda g, i: (0, 0)
    tile = lambda g, i: (g, i, 0)
    prev = lambda g, i: (g, jnp.maximum(i * hb - 1, 0), 0)
    nxt = lambda g, i: (g, jnp.minimum((i + 1) * hb, nh - 1), 0)
    tile_specs = [pl.BlockSpec((None, tm, D_MODEL), tile)]
    x_args, o_args = [x], [o]
    if tiles_per_seq > 1:
        tile_specs += [pl.BlockSpec((None, HALO, D_MODEL), prev),
                       pl.BlockSpec((None, HALO, D_MODEL), nxt)]
        x_args, o_args = [x] * 3, [o] * 3
    return pl.pallas_call(
        functools.partial(_block_ffn_kernel, tiles_per_seq=tiles_per_seq),
        grid=(g_n, l_n // tm),
        in_specs=tile_specs + tile_specs + [
            pl.BlockSpec((None, None, 1, 6 * D_MODEL), lambda g, i: (layer, row0 + g, 0, 0)),
            pl.BlockSpec(w_o.shape, const, pipeline_mode=RESIDENT),
            pl.BlockSpec((1, D_MODEL), const),
            pl.BlockSpec(w_up.shape, const, pipeline_mode=RESIDENT),
            pl.BlockSpec(conv_w.shape, const),
            pl.BlockSpec(conv_b.shape, const),
            pl.BlockSpec(w_down.shape, const, pipeline_mode=RESIDENT),
        ],
        out_specs=pl.BlockSpec((None, tm, D_MODEL), tile),
        out_shape=jax.ShapeDtypeStruct(x.shape, F32),
        scratch_shapes=[pltpu.VMEM((FF_CHUNK // LANES, tm + 2 * HALO, LANES), F32)]
        * (2 * FF_SLOTS),
        compiler_params=_cparams(2),
        name="proj_conv_ffn",
    )(*x_args, *o_args, mod4, w_o, nw, w_up, conv_w, conv_b, w_down)


def _rope_tables(l_n):
    half = HEAD_DIM // 4
    freqs = ROPE_BASE ** (-jnp.arange(half, dtype=F32) / half)
    t = jnp.arange(l_n)
    lane = jnp.arange(HEAD_DIM)
    pos = jnp.where(lane[None, :] < HEAD_DIM // 2, (t // GRID_W)[:, None], (t % GRID_W)[:, None])
    ang = pos.astype(F32) * freqs[lane % half][None, :]
    sign = jnp.where((lane % (2 * half)) < half, -1.0, 1.0).astype(F32)
    cos = jnp.cos(ang)
    sin = jnp.sin(ang) * sign[None, :]
    return jnp.tile(cos, (1, 2)), jnp.tile(sin, (1, 2))


def _nbr_bias_kernel(rpb_ref, o_ref, *, patterns):
    n_dr = 2 * WIN_R - 1
    n_dc = 2 * WIN_C - 1
    base = pl.program_id(0) * (n_dr * n_dc)
    qc = lax.broadcasted_iota(jnp.int32, (GRID_W, LANES), 0)
    lane = lax.broadcasted_iota(jnp.int32, (GRID_W, LANES), 1)
    kc = lane % GRID_W
    dc = jnp.clip(kc - qc, -(WIN_C - 1), WIN_C - 1) + WIN_C - 1
    cs = jnp.clip(qc - WIN_C // 2, 0, GRID_W - WIN_C)
    col_ok = (kc >= cs) & (kc < cs + WIN_C)
    masked = jnp.full((GRID_W, LANES), NEG, F32)
    hit = [col_ok & (dc == d) for d in range(n_dc)]
    blocks = []
    for a in range(n_dr):
        t = masked
        for d in range(n_dc):
            t = jnp.where(hit[d], rpb_ref[base + a * n_dc + d], t)
        blocks.append(t)
    pick = lambda a: masked if a is None else blocks[a]
    lo = lane < GRID_W
    for cls, pat in enumerate(patterns):
        for i in range(NBR_ROWS):
            for m in range(NBR_WIN // 2):
                o_ref[cls, i * GRID_W:(i + 1) * GRID_W, m * LANES:(m + 1) * LANES] = jnp.where(
                    lo, pick(pat[i][2 * m]), pick(pat[i][2 * m + 1]))


def _nbr_bias_table(rpb, rows):
    n_heads = rpb.shape[0]
    patterns = tuple(_nbr_pattern(r0, rows) for r0 in (0, NBR_ROWS, rows - NBR_ROWS))
    mq = NBR_ROWS * GRID_W
    nk = NBR_WIN * GRID_W
    return pl.pallas_call(
        functools.partial(_nbr_bias_kernel, patterns=patterns),
        grid=(n_heads,),
        in_specs=[pl.BlockSpec(memory_space=pltpu.SMEM)],
        out_specs=pl.BlockSpec((len(patterns), None, mq, nk), lambda h: (0, h, 0, 0)),
        out_shape=jax.ShapeDtypeStruct((len(patterns), n_heads, mq, nk), F32),
        compiler_params=_cparams(1),
        name="nbr_bias",
    )(rpb.astype(F32).reshape(-1))


def _block_diag_mean():
    r = jnp.arange(NORM_CHUNK) // HEAD_DIM
    return jnp.where(r[:, None] == r[None, :], 1.0 / HEAD_DIM, 0.0).astype(BF16)


def kernel(x_prompt, x_sample, cache_k_a, cache_v_a, cache_k_b, cache_v_b, c, c_ctx,
           norm_attn_w, norm_ffn_w, w_ada, b_ada,
           w_qkv_a, q_norm_a, k_norm_a, sink_a, w_o_a,
           w_qkv_b, q_norm_b, k_norm_b, rpb_b, w_o_b,
           w_up, conv_w, conv_b, w_down):
    depth = w_ada.shape[0]
    batch, seq, _ = x_prompt.shape
    dec_batch, dec_seq, _ = x_sample.shape
    nq = N_HEADS * HEAD_DIM

    cond = jnp.concatenate(
        [c_ctx[None, :], c, jnp.zeros((N_MOD_ROWS - 1 - dec_batch, D_MODEL), F32)], axis=0)
    mod = _ada_call(cond.T, 1 + dec_batch, w_ada, b_ada[:, None, :])
    mod4 = mod.reshape(depth, N_MOD_ROWS, 1, 6 * D_MODEL)

    bd = _block_diag_mean()
    rope_tabs = _rope_tables(dec_seq)
    tile_w = lambda w, n: jnp.tile(w, n)[None, :]

    xp = x_prompt.reshape(1, batch * seq, D_MODEL)
    xs = x_sample
    new_k_a, new_v_a, new_k_b, new_v_b = [], [], [], []
    for i in range(depth):
        j = i // 2
        nw_attn = norm_attn_w[i][None, :]
        nw_ffn = norm_ffn_w[i][None, :]
        if i % 2 == 0:
            nkv = N_KV_A * HEAD_DIM
            w_qkv = w_qkv_a[j].astype(BF16)
            w_o = w_o_a[j].astype(BF16)
            qn, kn = tile_w(q_norm_a[j], N_HEADS), tile_w(k_norm_a[j], N_KV_A)
            q, k, v, kd, vd = _qkv_call(xp, mod4, i, 0, nw_attn, w_qkv, qn, kn, bd, None,
                                        nkv=nkv, dup=True, f32_kv=True, tm=512)
            new_k_a.append(k.reshape(batch, seq, N_KV_A, HEAD_DIM))
            new_v_a.append(v.reshape(batch, seq, N_KV_A, HEAD_DIM))
            shp = (batch, seq, -1)
            op = _ctx_attn_call(q.reshape(shp), kd.reshape(shp), vd.reshape(shp), sink_a[j])
            q, kd, vd = _qkv_call(xs, mod4, i, 1, nw_attn, w_qkv, qn, kn, bd, rope_tabs,
                                  nkv=nkv, dup=True, f32_kv=False, tm=512)
            o_s = _win_attn_call(q, kd, vd,
                                 cache_k_a[:, j].reshape(dec_batch, -1, nkv),
                                 cache_v_a[:, j].reshape(dec_batch, -1, nkv), sink_a[j])
        else:
            nkv = nq
            w_qkv = w_qkv_b[j].astype(BF16)
            w_o = w_o_b[j].astype(BF16)
            qn, kn = tile_w(q_norm_b[j], N_HEADS), tile_w(k_norm_b[j], N_HEADS)
            q, k, v = _qkv_call(xp, mod4, i, 0, nw_attn, w_qkv, qn, kn, bd, None,
                                nkv=nkv, dup=False, f32_kv=True, tm=512)
            new_k_b.append(k.reshape(batch, seq, N_HEADS, HEAD_DIM))
            new_v_b.append(v.reshape(batch, seq, N_HEADS, HEAD_DIM))
            shp = (batch, seq, -1)
            op = _ctx_attn_call(q.reshape(shp), k.reshape(shp), v.reshape(shp), None)
            q, k, v = _qkv_call(xs, mod4, i, 1, nw_attn, w_qkv, qn, kn, bd, None,
                                nkv=nkv, dup=False, f32_kv=False, tm=512)
            o_s = _nbr_attn_call(q, k, v, _nbr_bias_table(rpb_b[j], dec_seq // GRID_W),
                                 cache_k_b[:, j].reshape(dec_batch, -1, nq),
                                 cache_v_b[:, j].reshape(dec_batch, -1, nq))
        blk_w = (w_o, nw_ffn, w_up[i].astype(BF16), conv_w[i], conv_b[i][None, :],
                 w_down[i].astype(BF16))
        xp = _block_ffn_call(xp, op.reshape(xp.shape), mod4, i, 0, *blk_w, seq_len=seq, tm=seq)
        xs = _block_ffn_call(xs, o_s, mod4, i, 1, *blk_w, seq_len=dec_seq, tm=512)
    return (xp.reshape(batch, seq, D_MODEL), xs,
            jnp.stack(new_k_a, axis=1), jnp.stack(new_v_a, axis=1),
            jnp.stack(new_k_b, axis=1), jnp.stack(new_v_b, axis=1))
```

```python
import functools

import jax
import jax.numpy as jnp
from jax import lax
from jax.experimental import pallas as pl
from jax.experimental.pallas import tpu as pltpu

D_MODEL = 1024
HEAD_DIM = 64
N_HEADS = 16
N_KV_A = 4
GRID_W = 64
WINDOW = 128
WIN_R = 8
WIN_C = 16
D_FF = 2816
ROPE_BASE = 10000.0
EPS = 1e-6
SCALE = HEAD_DIM ** -0.5

LANES = 128
HALO = 16
NBR_ROWS = 4
NBR_WIN = NBR_ROWS + WIN_R
NORM_CHUNK = 256
FF_CHUNK = 256
FF_SLOTS = 3
NEG = -1e30
N_MOD_ROWS = 8

RESIDENT = pl.Buffered(1)

F32 = jnp.float32
BF16 = jnp.bfloat16
VMEM_LIMIT = 56 * 1024 * 1024


def _cparams(n_axes):
    return pltpu.CompilerParams(
        dimension_semantics=("arbitrary",) * n_axes, vmem_limit_bytes=VMEM_LIMIT)


def _silu(x):
    return x * (1.0 / (1.0 + jnp.exp(-x)))


def _rms_modulate(x, nw, shift, scale):
    ms = jnp.mean(x * x, axis=-1, keepdims=True)
    y = x * lax.rsqrt(ms + EPS) * nw
    return y * (1.0 + scale) + shift


def _ada_kernel(ct_ref, w_ref, b_ref, o_ref, *, n_cond):
    a = _silu(ct_ref[...])
    w = w_ref[...]
    rows = [jnp.sum(a[:, r:r + 1] * w, axis=0, keepdims=True) for r in range(n_cond)]
    rows.append(jnp.zeros((N_MOD_ROWS - n_cond, w.shape[1]), F32))
    o_ref[...] = jnp.concatenate(rows, axis=0) + b_ref[...]


def _ada_call(cond_t, n_cond, w_ada, b_ada):
    depth = w_ada.shape[0]
    tn = 1024
    return pl.pallas_call(
        functools.partial(_ada_kernel, n_cond=n_cond),
        grid=(depth, 6 * D_MODEL // tn),
        in_specs=[
            pl.BlockSpec((D_MODEL, N_MOD_ROWS), lambda l, n: (0, 0)),
            pl.BlockSpec((None, D_MODEL, tn), lambda l, n: (l, 0, n)),
            pl.BlockSpec((None, 1, tn), lambda l, n: (l, 0, n)),
        ],
        out_specs=pl.BlockSpec((None, N_MOD_ROWS, tn), lambda l, n: (l, 0, n)),
        out_shape=jax.ShapeDtypeStruct((depth, N_MOD_ROWS, 6 * D_MODEL), F32),
        compiler_params=_cparams(2),
        name="adaln",
    )(cond_t, w_ada, b_ada)


def _head_rms(t, bd, w):
    outs = []
    for c in range(t.shape[1] // NORM_CHUNK):
        tc = t[:, c * NORM_CHUNK:(c + 1) * NORM_CHUNK]
        ms = jnp.dot((tc * tc).astype(BF16), bd, preferred_element_type=F32)
        outs.append(tc * lax.rsqrt(ms + EPS))
    return jnp.concatenate(outs, axis=1) * w


def _rope(t, cos, sin_signed, lo16):
    outs = []
    for c in range(t.shape[1] // LANES):
        tc = t[:, c * LANES:(c + 1) * LANES]
        partner = jnp.where(lo16, pltpu.roll(tc, LANES - 16, axis=1), pltpu.roll(tc, 16, axis=1))
        outs.append(tc * cos + partner * sin_signed)
    return jnp.concatenate(outs, axis=1)


def _dup_heads(t, lo64):
    outs = []
    for c in range(t.shape[1] // LANES):
        tc = t[:, c * LANES:(c + 1) * LANES]
        sw = pltpu.roll(tc, HEAD_DIM, axis=1)
        outs.append(jnp.where(lo64, tc, sw))
        outs.append(jnp.where(lo64, sw, tc))
    return jnp.concatenate(outs, axis=1)


def _qkv_kernel(*refs, nkv, rope, dup, f32_kv):
    x_ref, mod_ref, nw_ref, w_ref, qn_ref, kn_ref, bd_ref = refs[:7]
    pos = 7
    if rope:
        cos_ref, sin_ref = refs[pos:pos + 2]
        pos += 2
    outs = refs[pos:]
    nq = N_HEADS * HEAD_DIM
    lane = lax.broadcasted_iota(jnp.int32, (1, LANES), 1)

    h = _rms_modulate(x_ref[...], nw_ref[...], mod_ref[:, 0:D_MODEL],
                      mod_ref[:, D_MODEL:2 * D_MODEL])
    qkv = jnp.dot(h.astype(BF16), w_ref[...], preferred_element_type=F32)
    bd = bd_ref[...]
    q = _head_rms(qkv[:, :nq], bd, qn_ref[...])
    k = _head_rms(qkv[:, nq:nq + nkv], bd, kn_ref[...])
    v = qkv[:, nq + nkv:]
    if rope:
        lo16 = (lane & 16) == 0
        q = _rope(q, cos_ref[...], sin_ref[...], lo16)
        k = _rope(k, cos_ref[...], sin_ref[...], lo16)
    outs[0][...] = (q * SCALE).astype(BF16)
    o = 1
    if f32_kv:
        outs[o][...] = k
        outs[o + 1][...] = v
        o += 2
    if dup:
        lo64 = lane < HEAD_DIM
        k = _dup_heads(k, lo64)
        v = _dup_heads(v, lo64)
    if dup or not f32_kv:
        outs[o][...] = k.astype(BF16)
        outs[o + 1][...] = v.astype(BF16)


def _qkv_call(x, mod4, layer, row0, nw, w, qn, kn, bd, rope_tabs, *, nkv, dup, f32_kv, tm):
    g_n, l_n, _ = x.shape
    rope = rope_tabs is not None
    nq = N_HEADS * HEAD_DIM
    xmap = lambda g, i: (g, i, 0)
    const = lambda g, i: (0, 0)
    in_specs = [
        pl.BlockSpec((None, tm, D_MODEL), xmap),
        pl.BlockSpec((None, None, 1, 6 * D_MODEL), lambda g, i: (layer, row0 + g, 0, 0)),
        pl.BlockSpec((1, D_MODEL), const),
        pl.BlockSpec(w.shape, const),
        pl.BlockSpec((1, nq), const),
        pl.BlockSpec((1, nkv), const),
        pl.BlockSpec((NORM_CHUNK, NORM_CHUNK), const),
    ]
    args = [x, mod4, nw, w, qn, kn, bd]
    if rope:
        in_specs += [pl.BlockSpec((tm, LANES), lambda g, i: (i, 0))] * 2
        args += list(rope_tabs)
    out_shape = [jax.ShapeDtypeStruct((g_n, l_n, nq), BF16)]
    out_specs = [pl.BlockSpec((None, tm, nq), xmap)]
    if f32_kv:
        out_shape += [jax.ShapeDtypeStruct((g_n, l_n, nkv), F32)] * 2
        out_specs += [pl.BlockSpec((None, tm, nkv), xmap)] * 2
    if dup or not f32_kv:
        wkv = nkv * 2 if dup else nkv
        out_shape += [jax.ShapeDtypeStruct((g_n, l_n, wkv), BF16)] * 2
        out_specs += [pl.BlockSpec((None, tm, wkv), xmap)] * 2
    return pl.pallas_call(
        functools.partial(_qkv_kernel, nkv=nkv, rope=rope, dup=dup, f32_kv=f32_kv),
        grid=(g_n, l_n // tm),
        in_specs=in_specs,
        out_specs=out_specs,
        out_shape=out_shape,
        compiler_params=_cparams(2),
        name="qkv",
    )(*args)


def _scores(q2_list, segs):
    lo = lax.broadcasted_iota(jnp.int32, (1, LANES), 1) < HEAD_DIM
    zero = jnp.zeros((), BF16)
    stacked = []
    for q2 in q2_list:
        stacked.append(jnp.where(lo, q2, zero))
        stacked.append(jnp.where(lo, zero, q2))
    qs = jnp.concatenate(stacked, axis=0)
    scores = []
    for k2, _, bias in segs:
        s = lax.dot_general(qs, k2, (((1,), (1,)), ((), ())), preferred_element_type=F32)
        scores.append(s if bias is None else s + bias)
    return scores


def _softmax_pv(scores, segs, sinks, m_rows):
    lo = lax.broadcasted_iota(jnp.int32, (1, LANES), 1) < HEAD_DIM
    lane_tiles = lambda t: [t[:, c * LANES:(c + 1) * LANES] for c in range(t.shape[1] // LANES)]
    mx_tile = functools.reduce(jnp.maximum, [t for s in scores for t in lane_tiles(s)])
    if sinks is not None:
        sink_tile = jnp.concatenate(
            [jnp.full((m_rows, LANES), sk, F32) for sk in sinks], axis=0)
        mx_tile = jnp.maximum(mx_tile, sink_tile)
    mx = jnp.max(mx_tile, axis=-1, keepdims=True)
    den_tile = (jnp.exp(sink_tile - mx) * (1.0 / LANES) if sinks is not None
                else jnp.zeros((scores[0].shape[0], LANES), F32))
    acc = jnp.zeros((scores[0].shape[0], LANES), F32)
    for s, (_, v2, _) in zip(scores, segs):
        p = jnp.exp(s - mx)
        den_tile = functools.reduce(jnp.add, lane_tiles(p), den_tile)
        acc = acc + jnp.dot(p.astype(BF16), v2, preferred_element_type=F32)
    rows = lambda t, n: t[n * m_rows:(n + 1) * m_rows]
    out = acc / jnp.sum(den_tile, axis=-1, keepdims=True)
    return [jnp.where(lo, rows(out, 2 * i), rows(out, 2 * i + 1))
            for i in range(out.shape[0] // (2 * m_rows))]


def _run_attention(jobs, lookahead=True):
    scores = _scores(*jobs[0][:2])
    for n, (q2_list, segs, sinks, store) in enumerate(jobs):
        more = n + 1 < len(jobs)
        if lookahead and more:
            nxt = _scores(*jobs[n + 1][:2])
        store(_softmax_pv(scores, segs, sinks, q2_list[0].shape[0]))
        if more:
            scores = nxt if lookahead else _scores(*jobs[n + 1][:2])


def _pair_slices():
    return [slice(j * LANES, (j + 1) * LANES) for j in range(N_HEADS // 2)]


def _pair_store(o_ref, sl):
    def _store(outs):
        o_ref[:, sl] = outs[0].astype(o_ref.dtype)
    return _store


def _dup_f32_to_bf16(t):
    lo64 = lax.broadcasted_iota(jnp.int32, (1, LANES), 1) < HEAD_DIM
    return _dup_heads(t, lo64).astype(BF16)


def _gqa_heads(q, segs_of_group, sink_ref, o_ref, lookahead):
    pairs_per_job = 2
    def store(pairs):
        def _store(outs):
            for p, out in zip(pairs, outs):
                o_ref[:, p * LANES:(p + 1) * LANES] = out.astype(o_ref.dtype)
        return _store

    jobs = []
    for g in range(N_KV_A):
        for first in range(2 * g, 2 * g + 2, pairs_per_job):
            pairs = list(range(first, first + pairs_per_job))
            jobs.append(([q[:, p * LANES:(p + 1) * LANES] for p in pairs], segs_of_group(g),
                         [sink_ref[2 * p + i] for p in pairs for i in range(2)], store(pairs)))
    _run_attention(jobs, lookahead)


def _ctx_attn_a_kernel(sink_ref, q_ref, k_ref, v_ref, o_ref):
    q = q_ref[...]
    k = k_ref[...]
    v = v_ref[...]
    tile = lambda t, g: t[:, g * LANES:(g + 1) * LANES]
    _gqa_heads(q, lambda g: [(tile(k, g), tile(v, g), None)], sink_ref, o_ref, True)


def _ctx_attn_b_kernel(q_ref, k_ref, v_ref, o_ref):
    q = q_ref[...]
    k = k_ref[...].astype(BF16)
    v = v_ref[...].astype(BF16)
    _run_attention([
        ([q[:, sl]], [(k[:, sl], v[:, sl], None)], None, _pair_store(o_ref, sl))
        for sl in _pair_slices()])


def _ctx_attn_call(q, k, v, sink):
    b_n, l_n, nq = q.shape
    bmap = lambda b: (b, 0, 0)
    in_specs = [pl.BlockSpec((None, l_n, nq), bmap),
                pl.BlockSpec((None, l_n, k.shape[2]), bmap),
                pl.BlockSpec((None, l_n, v.shape[2]), bmap)]
    args = [q, k, v]
    kern = _ctx_attn_b_kernel
    if sink is not None:
        in_specs = [pl.BlockSpec(memory_space=pltpu.SMEM)] + in_specs
        args = [sink] + args
        kern = _ctx_attn_a_kernel
    return pl.pallas_call(
        kern,
        grid=(b_n,),
        in_specs=in_specs,
        out_specs=pl.BlockSpec((None, l_n, nq), bmap),
        out_shape=jax.ShapeDtypeStruct((b_n, l_n, nq), BF16),
        compiler_params=_cparams(1),
        name="ctx_attn",
    )(*args)


def _win_attn_kernel(sink_ref, q_ref, kp_ref, kc_ref, kn_ref, vp_ref, vc_ref, vn_ref,
                     ck_ref, cv_ref, o_ref, kctx_ref, vctx_ref, bias_ref):
    i = pl.program_id(1)
    nb = pl.num_programs(1)
    blk = q_ref.shape[0]

    @pl.when(i == 0)
    def _():
        kctx_ref[...] = _dup_f32_to_bf16(ck_ref[...])
        vctx_ref[...] = _dup_f32_to_bf16(cv_ref[...])
        a = lax.broadcasted_iota(jnp.int32, bias_ref.shape[1:], 0) % blk
        j = lax.broadcasted_iota(jnp.int32, bias_ref.shape[1:], 1)
        band = (j >= a) & (j <= a + 2 * WINDOW)
        for cls, ok in enumerate((band & (j >= blk), band, band & (j < 2 * blk))):
            bias_ref[cls] = jnp.where(ok, 0.0, NEG).astype(F32)

    q = q_ref[...]
    k_loc = jnp.concatenate([kp_ref[...], kc_ref[...], kn_ref[...]], axis=0)
    v_loc = jnp.concatenate([vp_ref[...], vc_ref[...], vn_ref[...]], axis=0)
    bias = bias_ref[jnp.where(i == 0, 0, jnp.where(i == nb - 1, 2, 1))]
    tile = lambda t, g: t[:, g * LANES:(g + 1) * LANES]
    _gqa_heads(q, lambda g: [(tile(k_loc, g), tile(v_loc, g), bias),
                             (kctx_ref[:, g * LANES:(g + 1) * LANES],
                              vctx_ref[:, g * LANES:(g + 1) * LANES], None)],
               sink_ref, o_ref, True)


def _win_attn_call(q, kd, vd, ck, cv, sink):
    b_n, l_n, nq = q.shape
    blk = WINDOW
    nb = l_n // blk
    assert nb >= 3
    wkv = kd.shape[2]
    prev = lambda b, i: (b, jnp.maximum(i - 1, 0), 0)
    cur = lambda b, i: (b, i, 0)
    nxt = lambda b, i: (b, jnp.minimum(i + 1, nb - 1), 0)
    kv_specs = [pl.BlockSpec((None, blk, wkv), m) for m in (prev, cur, nxt)]
    ctx_spec = pl.BlockSpec((None,) + ck.shape[1:], lambda b, i: (b, 0, 0))
    return pl.pallas_call(
        _win_attn_kernel,
        grid=(b_n, nb),
        in_specs=[pl.BlockSpec(memory_space=pltpu.SMEM),
                  pl.BlockSpec((None, blk, nq), cur)] + kv_specs + kv_specs + [ctx_spec, ctx_spec],
        out_specs=pl.BlockSpec((None, blk, nq), cur),
        out_shape=jax.ShapeDtypeStruct((b_n, l_n, nq), BF16),
        scratch_shapes=[pltpu.VMEM((ck.shape[1], wkv), BF16)] * 2
        + [pltpu.VMEM((3, N_HEADS // N_KV_A * blk, 3 * blk), F32)],
        compiler_params=_cparams(2),
        name="win_attn",
    )(sink, q, kd, kd, kd, vd, vd, vd, ck, cv)


def _nbr_window_start(r0, rows):
    return jnp.clip(r0 - WIN_R // 2, 0, rows - NBR_WIN)


def _nbr_pattern(r0, rows):
    ws = min(max(r0 - WIN_R // 2, 0), rows - NBR_WIN)
    pat = []
    for i in range(NBR_ROWS):
        r = r0 + i
        rs = min(max(r - WIN_R // 2, 0), rows - WIN_R)
        pat.append(tuple((ws + w - r + WIN_R - 1) if rs <= ws + w < rs + WIN_R else None
                         for w in range(NBR_WIN)))
    return tuple(pat)


def _nbr_class(rb, n_rb):
    return jnp.where(rb == 0, 0, jnp.where(rb == n_rb - 1, 2, 1))


def _nbr_attn_kernel(q_ref, k_ref, v_ref, bias_ref, kc_ref, vc_ref, o_ref):
    q = q_ref[...]
    _run_attention([
        ([q[:, sl]],
         [(k_ref[0, :, sl], v_ref[0, :, sl],
           jnp.concatenate([bias_ref[2 * j], bias_ref[2 * j + 1]], axis=0)),
          (kc_ref[:, sl], vc_ref[:, sl], None)],
         None, _pair_store(o_ref, sl))
        for j, sl in enumerate(_pair_slices())])


def _nbr_attn_call(q, k, v, bias_tab, ck, cv):
    b_n, l_n, nq = q.shape
    rows = l_n // GRID_W
    n_rb = rows // NBR_ROWS
    interior = _nbr_pattern(NBR_ROWS, rows)
    assert all(_nbr_pattern(rb * NBR_ROWS, rows) == interior for rb in range(1, n_rb - 1))
    mq = NBR_ROWS * GRID_W
    nk = NBR_WIN * GRID_W
    blk = lambda b, rb: (b, rb, 0)
    ctx_spec = pl.BlockSpec((None,) + ck.shape[1:], lambda b, rb: (b, 0, 0))
    kv_spec = pl.BlockSpec(
        (pl.Element(1), pl.Element(nk), pl.Element(nq)),
        lambda b, rb: (b, _nbr_window_start(rb * NBR_ROWS, rows) * GRID_W, 0))
    bias_spec = pl.BlockSpec((None, N_HEADS, mq, nk),
                             lambda b, rb: (_nbr_class(rb, n_rb), 0, 0, 0))
    return pl.pallas_call(
        _nbr_attn_kernel,
        grid=(b_n, n_rb),
        in_specs=[pl.BlockSpec((None, mq, nq), blk), kv_spec, kv_spec, bias_spec,
                  ctx_spec, ctx_spec],
        out_specs=pl.BlockSpec((None, mq, nq), blk),
        out_shape=jax.ShapeDtypeStruct((b_n, l_n, nq), BF16),
        compiler_params=_cparams(2),
        name="nbr_attn",
    )(q, k, v, bias_tab, ck, cv)


def _block_ffn_kernel(*refs, tiles_per_seq):
    halo = tiles_per_seq > 1
    if halo:
        x_ref, xp_ref, xn_ref, o_ref, op_ref, on_ref = refs[:6]
        refs = refs[6:]
    else:
        x_ref, o_ref = refs[:2]
        refs = refs[2:]
    mod_ref, wo_ref, nw_ref, wup_ref, cw_ref, cb_ref, wdn_ref, y_ref = refs[:8]
    u_refs = refs[8:]
    tm = x_ref.shape[0]
    n_slabs = FF_CHUNK // LANES
    gate_attn = mod_ref[:, 2 * D_MODEL:3 * D_MODEL]
    shift = mod_ref[:, 3 * D_MODEL:4 * D_MODEL]
    scale = mod_ref[:, 4 * D_MODEL:5 * D_MODEL]
    gate_ffn = mod_ref[:, 5 * D_MODEL:6 * D_MODEL]
    if halo:
        i = pl.program_id(1) % tiles_per_seq
        x_ext = jnp.concatenate([xp_ref[...], x_ref[...], xn_ref[...]], axis=0)
        o_ext = jnp.concatenate([op_ref[...], o_ref[...], on_ref[...]], axis=0)
        x1_ext = x_ext + gate_attn * jnp.dot(o_ext, wo_ref[...], preferred_element_type=F32)
        x1 = x1_ext[HALO:HALO + tm]
        row = lax.broadcasted_iota(jnp.int32, (tm + 2 * HALO, 1), 0)
        inside = ((row >= HALO) | (i > 0)) & ((row < HALO + tm) | (i < tiles_per_seq - 1))
        h = jnp.where(inside, _rms_modulate(x1_ext, nw_ref[...], shift, scale), 0.0).astype(BF16)
        u_rows = slice(None)
    else:
        x1 = x_ref[...] + gate_attn * jnp.dot(o_ref[...], wo_ref[...],
                                              preferred_element_type=F32)
        h = _rms_modulate(x1, nw_ref[...], shift, scale).astype(BF16)
        u_rows = pl.ds(HALO, tm)
        pad = jnp.zeros((8, LANES), F32)
        for u_ref in u_refs:
            for s in range(n_slabs):
                u_ref[s, pl.ds(HALO - 8, 8), :] = pad
                u_ref[s, pl.ds(HALO + tm, 8), :] = pad

    def conv(u_ref, col):
        outs = []
        for s in range(n_slabs):
            lanes = slice(col + s * LANES, col + (s + 1) * LANES)
            cw = cw_ref[:, lanes]
            out = cb_ref[:, lanes]
            for o in range(3):
                out = out + u_ref[s, pl.ds(HALO - 1 + o, tm), :] * cw[o:o + 1, :]
            outs.append(out)
        return jnp.concatenate(outs, axis=1)

    def up_one(u_ref, col):
        u = jnp.dot(h, wup_ref[:, col:col + FF_CHUNK], preferred_element_type=F32)
        for s in range(n_slabs):
            u_ref[s, u_rows, :] = u[:, s * LANES:(s + 1) * LANES]

    def up(c):
        up_one(u_refs[2 * (c % FF_SLOTS)], c * FF_CHUNK)
        up_one(u_refs[2 * (c % FF_SLOTS) + 1], D_FF + c * FF_CHUNK)

    n_chunks = D_FF // FF_CHUNK
    acc = jnp.zeros((tm, D_MODEL), F32)
    up(0)
    for c in range(n_chunks):
        if c + 1 < n_chunks:
            up(c + 1)
        ug_ref, uv_ref = u_refs[2 * (c % FF_SLOTS)], u_refs[2 * (c % FF_SLOTS) + 1]
        gc = c * FF_CHUNK
        act = (_silu(conv(ug_ref, gc)) * conv(uv_ref, D_FF + gc)).astype(BF16)
        acc = acc + jnp.dot(act, wdn_ref[gc:gc + FF_CHUNK, :], preferred_element_type=F32)
    y_ref[...] = x1 + gate_ffn * acc


def _block_ffn_call(x, o, mod4, layer, row0, w_o, nw, w_up, conv_w, conv_b, w_down, *,
                    seq_len, tm):
    g_n, l_n, _ = x.shape
    tiles_per_seq = seq_len // tm
    nh = l_n // HALO
    hb = tm // HALO
    const = lambda g, i: (0, 0)
    tile = lambda g, i: (g, i, 0)
    prev = lambda g, i: (g, jnp.maximum(i * hb - 1, 0), 0)
    nxt = lambda g, i: (g, jnp.minimum((i + 1) * hb, nh - 1), 0)
    tile_specs = [pl.BlockSpec((None, tm, D_MODEL), tile)]
    x_args, o_args = [x], [o]
    if tiles_per_seq > 1:
        tile_specs += [pl.BlockSpec((None, HALO, D_MODEL), prev),
                       pl.BlockSpec((None, HALO, D_MODEL), nxt)]
        x_args, o_args = [x] * 3, [o] * 3
    return pl.pallas_call(
        functools.partial(_block_ffn_kernel, tiles_per_seq=tiles_per_seq),
        grid=(g_n, l_n // tm),
        in_specs=tile_specs + tile_specs + [
            pl.BlockSpec((None, None, 1, 6 * D_MODEL), lambda g, i: (layer, row0 + g, 0, 0)),
            pl.BlockSpec(w_o.shape, const, pipeline_mode=RESIDENT),
            pl.BlockSpec((1, D_MODEL), const),
            pl.BlockSpec(w_up.shape, const, pipeline_mode=RESIDENT),
            pl.BlockSpec(conv_w.shape, const),
            pl.BlockSpec(conv_b.shape, const),
            pl.BlockSpec(w_down.shape, const, pipeline_mode=RESIDENT),
        ],
        out_specs=pl.BlockSpec((None, tm, D_MODEL), tile),
        out_shape=jax.ShapeDtypeStruct(x.shape, F32),
        scratch_shapes=[pltpu.VMEM((FF_CHUNK // LANES, tm + 2 * HALO, LANES), F32)]
        * (2 * FF_SLOTS),
        compiler_params=_cparams(2),
        name="proj_conv_ffn",
    )(*x_args, *o_args, mod4, w_o, nw, w_up, conv_w, conv_b, w_down)


def _rope_tables(l_n):
    half = HEAD_DIM // 4
    freqs = ROPE_BASE ** (-jnp.arange(half, dtype=F32) / half)
    t = jnp.arange(l_n)
    lane = jnp.arange(HEAD_DIM)
    pos = jnp.where(lane[None, :] < HEAD_DIM // 2, (t // GRID_W)[:, None], (t % GRID_W)[:, None])
    ang = pos.astype(F32) * freqs[lane % half][None, :]
    sign = jnp.where((lane % (2 * half)) < half, -1.0, 1.0).astype(F32)
    cos = jnp.cos(ang)
    sin = jnp.sin(ang) * sign[None, :]
    return jnp.tile(cos, (1, 2)), jnp.tile(sin, (1, 2))


def _nbr_bias_kernel(rpb_ref, o_ref, *, patterns):
    n_dr = 2 * WIN_R - 1
    n_dc = 2 * WIN_C - 1
    base = pl.program_id(0) * (n_dr * n_dc)
    qc = lax.broadcasted_iota(jnp.int32, (GRID_W, LANES), 0)
    lane = lax.broadcasted_iota(jnp.int32, (GRID_W, LANES), 1)
    kc = lane % GRID_W
    dc = jnp.clip(kc - qc, -(WIN_C - 1), WIN_C - 1) + WIN_C - 1
    cs = jnp.clip(qc - WIN_C // 2, 0, GRID_W - WIN_C)
    col_ok = (kc >= cs) & (kc < cs + WIN_C)
    masked = jnp.full((GRID_W, LANES), NEG, F32)
    hit = [col_ok & (dc == d) for d in range(n_dc)]
    blocks = []
    for a in range(n_dr):
        t = masked
        for d in range(n_dc):
            t = jnp.where(hit[d], rpb_ref[base + a * n_dc + d], t)
        blocks.append(t)
    pick = lambda a: masked if a is None else blocks[a]
    lo = lane < GRID_W
    for cls, pat in enumerate(patterns):
        for i in range(NBR_ROWS):
            for m in range(NBR_WIN // 2):
                o_ref[cls, i * GRID_W:(i + 1) * GRID_W, m * LANES:(m + 1) * LANES] = jnp.where(
                    lo, pick(pat[i][2 * m]), pick(pat[i][2 * m + 1]))


def _nbr_bias_table(rpb, rows):
    n_heads = rpb.shape[0]
    patterns = tuple(_nbr_pattern(r0, rows) for r0 in (0, NBR_ROWS, rows - NBR_ROWS))
    mq = NBR_ROWS * GRID_W
    nk = NBR_WIN * GRID_W
    return pl.pallas_call(
        functools.partial(_nbr_bias_kernel, patterns=patterns),
        grid=(n_heads,),
        in_specs=[pl.BlockSpec(memory_space=pltpu.SMEM)],
        out_specs=pl.BlockSpec((len(patterns), None, mq, nk), lambda h: (0, h, 0, 0)),
        out_shape=jax.ShapeDtypeStruct((len(patterns), n_heads, mq, nk), F32),
        compiler_params=_cparams(1),
        name="nbr_bias",
    )(rpb.astype(F32).reshape(-1))


def _block_diag_mean():
    r = jnp.arange(NORM_CHUNK) // HEAD_DIM
    return jnp.where(r[:, None] == r[None, :], 1.0 / HEAD_DIM, 0.0).astype(BF16)


def kernel(x_prompt, x_sample, cache_k_a, cache_v_a, cache_k_b, cache_v_b, c, c_ctx,
           norm_attn_w, norm_ffn_w, w_ada, b_ada,
           w_qkv_a, q_norm_a, k_norm_a, sink_a, w_o_a,
           w_qkv_b, q_norm_b, k_norm_b, rpb_b, w_o_b,
           w_up, conv_w, conv_b, w_down):
    depth = w_ada.shape[0]
    batch, seq, _ = x_prompt.shape
    dec_batch, dec_seq, _ = x_sample.shape
    nq = N_HEADS * HEAD_DIM

    cond = jnp.concatenate(
        [c_ctx[None, :], c, jnp.zeros((N_MOD_ROWS - 1 - dec_batch, D_MODEL), F32)], axis=0)
    mod = _ada_call(cond.T, 1 + dec_batch, w_ada, b_ada[:, None, :])
    mod4 = mod.reshape(depth, N_MOD_ROWS, 1, 6 * D_MODEL)

    bd = _block_diag_mean()
    rope_tabs = _rope_tables(dec_seq)
    tile_w = lambda w, n: jnp.tile(w, n)[None, :]

    xp = x_prompt.reshape(1, batch * seq, D_MODEL)
    xs = x_sample
    new_k_a, new_v_a, new_k_b, new_v_b = [], [], [], []
    for i in range(depth):
        j = i // 2
        nw_attn = norm_attn_w[i][None, :]
        nw_ffn = norm_ffn_w[i][None, :]
        if i % 2 == 0:
            nkv = N_KV_A * HEAD_DIM
            w_qkv = w_qkv_a[j].astype(BF16)
            w_o = w_o_a[j].astype(BF16)
            qn, kn = tile_w(q_norm_a[j], N_HEADS), tile_w(k_norm_a[j], N_KV_A)
            q, k, v, kd, vd = _qkv_call(xp, mod4, i, 0, nw_attn, w_qkv, qn, kn, bd, None,
                                        nkv=nkv, dup=True, f32_kv=True, tm=512)
            new_k_a.append(k.reshape(batch, seq, N_KV_A, HEAD_DIM))
            new_v_a.append(v.reshape(batch, seq, N_KV_A, HEAD_DIM))
            shp = (batch, seq, -1)
            op = _ctx_attn_call(q.reshape(shp), kd.reshape(shp), vd.reshape(shp), sink_a[j])
            q, kd, vd = _qkv_call(xs, mod4, i, 1, nw_attn, w_qkv, qn, kn, bd, rope_tabs,
                                  nkv=nkv, dup=True, f32_kv=False, tm=512)
            o_s = _win_attn_call(q, kd, vd,
                                 cache_k_a[:, j].reshape(dec_batch, -1, nkv),
                                 cache_v_a[:, j].reshape(dec_batch, -1, nkv), sink_a[j])
        else:
            nkv = nq
            w_qkv = w_qkv_b[j].astype(BF16)
            w_o = w_o_b[j].astype(BF16)
            qn, kn = tile_w(q_norm_b[j], N_HEADS), tile_w(k_norm_b[j], N_HEADS)
            q, k, v = _qkv_call(xp, mod4, i, 0, nw_attn, w_qkv, qn, kn, bd, None,
                                nkv=nkv, dup=False, f32_kv=True, tm=512)
            new_k_b.append(k.reshape(batch, seq, N_HEADS, HEAD_DIM))
            new_v_b.append(v.reshape(batch, seq, N_HEADS, HEAD_DIM))
            shp = (batch, seq, -1)
            op = _ctx_attn_call(q.reshape(shp), k.reshape(shp), v.reshape(shp), None)
            q, k, v = _qkv_call(xs, mod4, i, 1, nw_attn, w_qkv, qn, kn, bd, None,
                                nkv=nkv, dup=False, f32_kv=False, tm=512)
            o_s = _nbr_attn_call(q, k, v, _nbr_bias_table(rpb_b[j], dec_seq // GRID_W),
                                 cache_k_b[:, j].reshape(dec_batch, -1, nq).astype(BF16),
                                 cache_v_b[:, j].reshape(dec_batch, -1, nq).astype(BF16))
        blk_w = (w_o, nw_ffn, w_up[i].astype(BF16), conv_w[i], conv_b[i][None, :],
                 w_down[i].astype(BF16))
        xp = _block_ffn_call(xp, op.reshape(xp.shape), mod4, i, 0, *blk_w, seq_len=seq, tm=seq)
        xs = _block_ffn_call(xs, o_s, mod4, i, 1, *blk_w, seq_len=dec_seq, tm=512)
    return (xp.reshape(batch, seq, D_MODEL), xs,
            jnp.stack(new_k_a, axis=1), jnp.stack(new_v_a, axis=1),
            jnp.stack(new_k_b, axis=1), jnp.stack(new_v_b, axis=1))
```

```python
import functools

import jax
import jax.numpy as jnp
from jax import lax
from jax.experimental import pallas as pl
from jax.experimental.pallas import tpu as pltpu

D_MODEL = 1024
HEAD_DIM = 64
N_HEADS = 16
N_KV_A = 4
GRID_W = 64
WINDOW = 128
WIN_R = 8
WIN_C = 16
D_FF = 2816
ROPE_BASE = 10000.0
EPS = 1e-6
SCALE = HEAD_DIM ** -0.5

LANES = 128
HALO = 16
NBR_ROWS = 4
NBR_WIN = NBR_ROWS + WIN_R
NORM_CHUNK = 256
FF_CHUNK = 256
FF_SLOTS = 3
NEG = -1e30
N_MOD_ROWS = 8

RESIDENT = pl.Buffered(1)

F32 = jnp.float32
BF16 = jnp.bfloat16
VMEM_LIMIT = 56 * 1024 * 1024


def _cparams(n_axes):
    return pltpu.CompilerParams(
        dimension_semantics=("arbitrary",) * n_axes, vmem_limit_bytes=VMEM_LIMIT)


def _silu(x):
    return x * (1.0 / (1.0 + jnp.exp(-x)))


def _rms_modulate(x, nw, shift, scale):
    ms = jnp.mean(x * x, axis=-1, keepdims=True)
    y = x * lax.rsqrt(ms + EPS) * nw
    return y * (1.0 + scale) + shift


def _ada_kernel(ct_ref, w_ref, b_ref, o_ref, *, n_cond):
    a = _silu(ct_ref[...])
    w = w_ref[...]
    rows = [jnp.sum(a[:, r:r + 1] * w, axis=0, keepdims=True) for r in range(n_cond)]
    rows.append(jnp.zeros((N_MOD_ROWS - n_cond, w.shape[1]), F32))
    o_ref[...] = jnp.concatenate(rows, axis=0) + b_ref[...]


def _ada_call(cond_t, n_cond, w_ada, b_ada):
    depth = w_ada.shape[0]
    tn = 2048
    return pl.pallas_call(
        functools.partial(_ada_kernel, n_cond=n_cond),
        grid=(depth, 6 * D_MODEL // tn),
        in_specs=[
            pl.BlockSpec((D_MODEL, N_MOD_ROWS), lambda l, n: (0, 0)),
            pl.BlockSpec((None, D_MODEL, tn), lambda l, n: (l, 0, n)),
            pl.BlockSpec((None, 1, tn), lambda l, n: (l, 0, n)),
        ],
        out_specs=pl.BlockSpec((None, N_MOD_ROWS, tn), lambda l, n: (l, 0, n)),
        out_shape=jax.ShapeDtypeStruct((depth, N_MOD_ROWS, 6 * D_MODEL), F32),
        compiler_params=_cparams(2),
        name="adaln",
    )(cond_t, w_ada, b_ada)


def _head_rms(t, bd, w):
    outs = []
    for c in range(t.shape[1] // NORM_CHUNK):
        tc = t[:, c * NORM_CHUNK:(c + 1) * NORM_CHUNK]
        ms = jnp.dot((tc * tc).astype(BF16), bd, preferred_element_type=F32)
        outs.append(tc * lax.rsqrt(ms + EPS))
    return jnp.concatenate(outs, axis=1) * w


def _rope(t, cos, sin_signed, lo16):
    outs = []
    for c in range(t.shape[1] // LANES):
        tc = t[:, c * LANES:(c + 1) * LANES]
        partner = jnp.where(lo16, pltpu.roll(tc, LANES - 16, axis=1), pltpu.roll(tc, 16, axis=1))
        outs.append(tc * cos + partner * sin_signed)
    return jnp.concatenate(outs, axis=1)


def _dup_heads(t, lo64):
    outs = []
    for c in range(t.shape[1] // LANES):
        tc = t[:, c * LANES:(c + 1) * LANES]
        sw = pltpu.roll(tc, HEAD_DIM, axis=1)
        outs.append(jnp.where(lo64, tc, sw))
        outs.append(jnp.where(lo64, sw, tc))
    return jnp.concatenate(outs, axis=1)


def _qkv_kernel(*refs, nkv, rope, dup, f32_kv):
    x_ref, mod_ref, nw_ref, w_ref, qn_ref, kn_ref, bd_ref = refs[:7]
    pos = 7
    if rope:
        cos_ref, sin_ref = refs[pos:pos + 2]
        pos += 2
    outs = refs[pos:]
    nq = N_HEADS * HEAD_DIM
    lane = lax.broadcasted_iota(jnp.int32, (1, LANES), 1)

    h = _rms_modulate(x_ref[...], nw_ref[...], mod_ref[:, 0:D_MODEL],
                      mod_ref[:, D_MODEL:2 * D_MODEL])
    qkv = jnp.dot(h.astype(BF16), w_ref[...], preferred_element_type=F32)
    bd = bd_ref[...]
    q = _head_rms(qkv[:, :nq], bd, qn_ref[...])
    k = _head_rms(qkv[:, nq:nq + nkv], bd, kn_ref[...])
    v = qkv[:, nq + nkv:]
    if rope:
        lo16 = (lane & 16) == 0
        q = _rope(q, cos_ref[...], sin_ref[...], lo16)
        k = _rope(k, cos_ref[...], sin_ref[...], lo16)
    outs[0][...] = (q * SCALE).astype(BF16)
    o = 1
    if f32_kv:
        outs[o][...] = k
        outs[o + 1][...] = v
        o += 2
    if dup:
        lo64 = lane < HEAD_DIM
        k = _dup_heads(k, lo64)
        v = _dup_heads(v, lo64)
    if dup or not f32_kv:
        outs[o][...] = k.astype(BF16)
        outs[o + 1][...] = v.astype(BF16)


def _qkv_call(x, mod4, layer, row0, nw, w, qn, kn, bd, rope_tabs, *, nkv, dup, f32_kv, tm):
    g_n, l_n, _ = x.shape
    rope = rope_tabs is not None
    nq = N_HEADS * HEAD_DIM
    xmap = lambda g, i: (g, i, 0)
    const = lambda g, i: (0, 0)
    in_specs = [
        pl.BlockSpec((None, tm, D_MODEL), xmap),
        pl.BlockSpec((None, None, 1, 6 * D_MODEL), lambda g, i: (layer, row0 + g, 0, 0)),
        pl.BlockSpec((1, D_MODEL), const),
        pl.BlockSpec(w.shape, const),
        pl.BlockSpec((1, nq), const),
        pl.BlockSpec((1, nkv), const),
        pl.BlockSpec((NORM_CHUNK, NORM_CHUNK), const),
    ]
    args = [x, mod4, nw, w, qn, kn, bd]
    if rope:
        in_specs += [pl.BlockSpec((tm, LANES), lambda g, i: (i, 0))] * 2
        args += list(rope_tabs)
    out_shape = [jax.ShapeDtypeStruct((g_n, l_n, nq), BF16)]
    out_specs = [pl.BlockSpec((None, tm, nq), xmap)]
    if f32_kv:
        out_shape += [jax.ShapeDtypeStruct((g_n, l_n, nkv), F32)] * 2
        out_specs += [pl.BlockSpec((None, tm, nkv), xmap)] * 2
    if dup or not f32_kv:
        wkv = nkv * 2 if dup else nkv
        out_shape += [jax.ShapeDtypeStruct((g_n, l_n, wkv), BF16)] * 2
        out_specs += [pl.BlockSpec((None, tm, wkv), xmap)] * 2
    return pl.pallas_call(
        functools.partial(_qkv_kernel, nkv=nkv, rope=rope, dup=dup, f32_kv=f32_kv),
        grid=(g_n, l_n // tm),
        in_specs=in_specs,
        out_specs=out_specs,
        out_shape=out_shape,
        compiler_params=_cparams(2),
        name="qkv",
    )(*args)


def _scores(q2_list, segs):
    lo = lax.broadcasted_iota(jnp.int32, (1, LANES), 1) < HEAD_DIM
    zero = jnp.zeros((), BF16)
    stacked = []
    for q2 in q2_list:
        stacked.append(jnp.where(lo, q2, zero))
        stacked.append(jnp.where(lo, zero, q2))
    qs = jnp.concatenate(stacked, axis=0)
    scores = []
    for k2, _, bias in segs:
        s = lax.dot_general(qs, k2, (((1,), (1,)), ((), ())), preferred_element_type=F32)
        scores.append(s if bias is None else s + bias)
    return scores


def _softmax_pv(scores, segs, sinks, m_rows):
    lo = lax.broadcasted_iota(jnp.int32, (1, LANES), 1) < HEAD_DIM
    lane_tiles = lambda t: [t[:, c * LANES:(c + 1) * LANES] for c in range(t.shape[1] // LANES)]
    mx_tile = functools.reduce(jnp.maximum, [t for s in scores for t in lane_tiles(s)])
    if sinks is not None:
        sink_tile = jnp.concatenate(
            [jnp.full((m_rows, LANES), sk, F32) for sk in sinks], axis=0)
        mx_tile = jnp.maximum(mx_tile, sink_tile)
    mx = jnp.max(mx_tile, axis=-1, keepdims=True)
    den_tile = (jnp.exp(sink_tile - mx) * (1.0 / LANES) if sinks is not None
                else jnp.zeros((scores[0].shape[0], LANES), F32))
    acc = jnp.zeros((scores[0].shape[0], LANES), F32)
    for s, (_, v2, _) in zip(scores, segs):
        p = jnp.exp(s - mx)
        den_tile = functools.reduce(jnp.add, lane_tiles(p), den_tile)
        acc = acc + jnp.dot(p.astype(BF16), v2, preferred_element_type=F32)
    rows = lambda t, n: t[n * m_rows:(n + 1) * m_rows]
    out = acc / jnp.sum(den_tile, axis=-1, keepdims=True)
    return [jnp.where(lo, rows(out, 2 * i), rows(out, 2 * i + 1))
            for i in range(out.shape[0] // (2 * m_rows))]


def _run_attention(jobs, lookahead=True):
    scores = _scores(*jobs[0][:2])
    for n, (q2_list, segs, sinks, store) in enumerate(jobs):
        more = n + 1 < len(jobs)
        if lookahead and more:
            nxt = _scores(*jobs[n + 1][:2])
        store(_softmax_pv(scores, segs, sinks, q2_list[0].shape[0]))
        if more:
            scores = nxt if lookahead else _scores(*jobs[n + 1][:2])


def _pair_slices():
    return [slice(j * LANES, (j + 1) * LANES) for j in range(N_HEADS // 2)]


def _pair_store(o_ref, sl):
    def _store(outs):
        o_ref[:, sl] = outs[0].astype(o_ref.dtype)
    return _store


def _dup_f32_to_bf16(t):
    lo64 = lax.broadcasted_iota(jnp.int32, (1, LANES), 1) < HEAD_DIM
    return _dup_heads(t, lo64).astype(BF16)


def _gqa_heads(q, segs_of_group, sink_ref, o_ref, lookahead):
    pairs_per_job = 2
    def store(pairs):
        def _store(outs):
            for p, out in zip(pairs, outs):
                o_ref[:, p * LANES:(p + 1) * LANES] = out.astype(o_ref.dtype)
        return _store

    jobs = []
    for g in range(N_KV_A):
        for first in range(2 * g, 2 * g + 2, pairs_per_job):
            pairs = list(range(first, first + pairs_per_job))
            jobs.append(([q[:, p * LANES:(p + 1) * LANES] for p in pairs], segs_of_group(g),
                         [sink_ref[2 * p + i] for p in pairs for i in range(2)], store(pairs)))
    _run_attention(jobs, lookahead)


def _ctx_attn_a_kernel(sink_ref, q_ref, k_ref, v_ref, o_ref):
    q = q_ref[...]
    k = k_ref[...]
    v = v_ref[...]
    tile = lambda t, g: t[:, g * LANES:(g + 1) * LANES]
    _gqa_heads(q, lambda g: [(tile(k, g), tile(v, g), None)], sink_ref, o_ref, True)


def _ctx_attn_b_kernel(q_ref, k_ref, v_ref, o_ref):
    q = q_ref[...]
    k = k_ref[...].astype(BF16)
    v = v_ref[...].astype(BF16)
    _run_attention([
        ([q[:, sl]], [(k[:, sl], v[:, sl], None)], None, _pair_store(o_ref, sl))
        for sl in _pair_slices()])


def _ctx_attn_call(q, k, v, sink):
    b_n, l_n, nq = q.shape
    bmap = lambda b: (b, 0, 0)
    in_specs = [pl.BlockSpec((None, l_n, nq), bmap),
                pl.BlockSpec((None, l_n, k.shape[2]), bmap),
                pl.BlockSpec((None, l_n, v.shape[2]), bmap)]
    args = [q, k, v]
    kern = _ctx_attn_b_kernel
    if sink is not None:
        in_specs = [pl.BlockSpec(memory_space=pltpu.SMEM)] + in_specs
        args = [sink] + args
        kern = _ctx_attn_a_kernel
    return pl.pallas_call(
        kern,
        grid=(b_n,),
        in_specs=in_specs,
        out_specs=pl.BlockSpec((None, l_n, nq), bmap),
        out_shape=jax.ShapeDtypeStruct((b_n, l_n, nq), BF16),
        compiler_params=_cparams(1),
        name="ctx_attn",
    )(*args)


def _win_attn_kernel(sink_ref, q_ref, kp_ref, kc_ref, kn_ref, vp_ref, vc_ref, vn_ref,
                     ck_ref, cv_ref, o_ref, kctx_ref, vctx_ref, bias_ref):
    i = pl.program_id(1)
    nb = pl.num_programs(1)
    blk = q_ref.shape[0]

    @pl.when(i == 0)
    def _():
        kctx_ref[...] = _dup_f32_to_bf16(ck_ref[...])
        vctx_ref[...] = _dup_f32_to_bf16(cv_ref[...])
        a = lax.broadcasted_iota(jnp.int32, bias_ref.shape[1:], 0) % blk
        j = lax.broadcasted_iota(jnp.int32, bias_ref.shape[1:], 1)
        band = (j >= a) & (j <= a + 2 * WINDOW)
        for cls, ok in enumerate((band & (j >= blk), band, band & (j < 2 * blk))):
            bias_ref[cls] = jnp.where(ok, 0.0, NEG).astype(F32)

    q = q_ref[...]
    k_loc = jnp.concatenate([kp_ref[...], kc_ref[...], kn_ref[...]], axis=0)
    v_loc = jnp.concatenate([vp_ref[...], vc_ref[...], vn_ref[...]], axis=0)
    bias = bias_ref[jnp.where(i == 0, 0, jnp.where(i == nb - 1, 2, 1))]
    tile = lambda t, g: t[:, g * LANES:(g + 1) * LANES]
    _gqa_heads(q, lambda g: [(tile(k_loc, g), tile(v_loc, g), bias),
                             (kctx_ref[:, g * LANES:(g + 1) * LANES],
                              vctx_ref[:, g * LANES:(g + 1) * LANES], None)],
               sink_ref, o_ref, True)


def _win_attn_call(q, kd, vd, ck, cv, sink):
    b_n, l_n, nq = q.shape
    blk = WINDOW
    nb = l_n // blk
    assert nb >= 3
    wkv = kd.shape[2]
    prev = lambda b, i: (b, jnp.maximum(i - 1, 0), 0)
    cur = lambda b, i: (b, i, 0)
    nxt = lambda b, i: (b, jnp.minimum(i + 1, nb - 1), 0)
    kv_specs = [pl.BlockSpec((None, blk, wkv), m) for m in (prev, cur, nxt)]
    ctx_spec = pl.BlockSpec((None,) + ck.shape[1:], lambda b, i: (b, 0, 0))
    return pl.pallas_call(
        _win_attn_kernel,
        grid=(b_n, nb),
        in_specs=[pl.BlockSpec(memory_space=pltpu.SMEM),
                  pl.BlockSpec((None, blk, nq), cur)] + kv_specs + kv_specs + [ctx_spec, ctx_spec],
        out_specs=pl.BlockSpec((None, blk, nq), cur),
        out_shape=jax.ShapeDtypeStruct((b_n, l_n, nq), BF16),
        scratch_shapes=[pltpu.VMEM((ck.shape[1], wkv), BF16)] * 2
        + [pltpu.VMEM((3, N_HEADS // N_KV_A * blk, 3 * blk), F32)],
        compiler_params=_cparams(2),
        name="win_attn",
    )(sink, q, kd, kd, kd, vd, vd, vd, ck, cv)


def _nbr_window_start(r0, rows):
    return jnp.clip(r0 - WIN_R // 2, 0, rows - NBR_WIN)


def _nbr_pattern(r0, rows):
    ws = min(max(r0 - WIN_R // 2, 0), rows - NBR_WIN)
    pat = []
    for i in range(NBR_ROWS):
        r = r0 + i
        rs = min(max(r - WIN_R // 2, 0), rows - WIN_R)
        pat.append(tuple((ws + w - r + WIN_R - 1) if rs <= ws + w < rs + WIN_R else None
                         for w in range(NBR_WIN)))
    return tuple(pat)


def _nbr_class(rb, n_rb):
    return jnp.where(rb == 0, 0, jnp.where(rb == n_rb - 1, 2, 1))


def _nbr_attn_kernel(q_ref, k_ref, v_ref, bias_ref, kc_ref, vc_ref, o_ref):
    q = q_ref[...]
    _run_attention([
        ([q[:, sl]],
         [(k_ref[0, :, sl], v_ref[0, :, sl],
           jnp.concatenate([bias_ref[2 * j], bias_ref[2 * j + 1]], axis=0)),
          (kc_ref[:, sl], vc_ref[:, sl], None)],
         None, _pair_store(o_ref, sl))
        for j, sl in enumerate(_pair_slices())])


def _nbr_attn_call(q, k, v, bias_tab, ck, cv):
    b_n, l_n, nq = q.shape
    rows = l_n // GRID_W
    n_rb = rows // NBR_ROWS
    interior = _nbr_pattern(NBR_ROWS, rows)
    assert all(_nbr_pattern(rb * NBR_ROWS, rows) == interior for rb in range(1, n_rb - 1))
    mq = NBR_ROWS * GRID_W
    nk = NBR_WIN * GRID_W
    blk = lambda b, rb: (b, rb, 0)
    ctx_spec = pl.BlockSpec((None,) + ck.shape[1:], lambda b, rb: (b, 0, 0))
    kv_spec = pl.BlockSpec(
        (pl.Element(1), pl.Element(nk), pl.Element(nq)),
        lambda b, rb: (b, _nbr_window_start(rb * NBR_ROWS, rows) * GRID_W, 0))
    bias_spec = pl.BlockSpec((None, N_HEADS, mq, nk),
                             lambda b, rb: (_nbr_class(rb, n_rb), 0, 0, 0))
    return pl.pallas_call(
        _nbr_attn_kernel,
        grid=(b_n, n_rb),
        in_specs=[pl.BlockSpec((None, mq, nq), blk), kv_spec, kv_spec, bias_spec,
                  ctx_spec, ctx_spec],
        out_specs=pl.BlockSpec((None, mq, nq), blk),
        out_shape=jax.ShapeDtypeStruct((b_n, l_n, nq), BF16),
        compiler_params=_cparams(2),
        name="nbr_attn",
    )(q, k, v, bias_tab, ck, cv)


def _block_ffn_kernel(*refs, tiles_per_seq):
    halo = tiles_per_seq > 1
    if halo:
        x_ref, xp_ref, xn_ref, o_ref, op_ref, on_ref = refs[:6]
        refs = refs[6:]
    else:
        x_ref, o_ref = refs[:2]
        refs = refs[2:]
    mod_ref, wo_ref, nw_ref, wup_ref, cw_ref, cb_ref, wdn_ref, y_ref = refs[:8]
    u_refs = refs[8:]
    tm = x_ref.shape[0]
    n_slabs = FF_CHUNK // LANES
    gate_attn = mod_ref[:, 2 * D_MODEL:3 * D_MODEL]
    shift = mod_ref[:, 3 * D_MODEL:4 * D_MODEL]
    scale = mod_ref[:, 4 * D_MODEL:5 * D_MODEL]
    gate_ffn = mod_ref[:, 5 * D_MODEL:6 * D_MODEL]
    if halo:
        i = pl.program_id(1) % tiles_per_seq
        x_ext = jnp.concatenate([xp_ref[...], x_ref[...], xn_ref[...]], axis=0)
        o_ext = jnp.concatenate([op_ref[...], o_ref[...], on_ref[...]], axis=0)
        x1_ext = x_ext + gate_attn * jnp.dot(o_ext, wo_ref[...], preferred_element_type=F32)
        x1 = x1_ext[HALO:HALO + tm]
        row = lax.broadcasted_iota(jnp.int32, (tm + 2 * HALO, 1), 0)
        inside = ((row >= HALO) | (i > 0)) & ((row < HALO + tm) | (i < tiles_per_seq - 1))
        h = jnp.where(inside, _rms_modulate(x1_ext, nw_ref[...], shift, scale), 0.0).astype(BF16)
        u_rows = slice(None)
    else:
        x1 = x_ref[...] + gate_attn * jnp.dot(o_ref[...], wo_ref[...],
                                              preferred_element_type=F32)
        h = _rms_modulate(x1, nw_ref[...], shift, scale).astype(BF16)
        u_rows = pl.ds(HALO, tm)
        pad = jnp.zeros((8, LANES), F32)
        for u_ref in u_refs:
            for s in range(n_slabs):
                u_ref[s, pl.ds(HALO - 8, 8), :] = pad
                u_ref[s, pl.ds(HALO + tm, 8), :] = pad

    def conv(u_ref, col):
        outs = []
        for s in range(n_slabs):
            lanes = slice(col + s * LANES, col + (s + 1) * LANES)
            cw = cw_ref[:, lanes]
            out = cb_ref[:, lanes]
            for o in range(3):
                out = out + u_ref[s, pl.ds(HALO - 1 + o, tm), :] * cw[o:o + 1, :]
            outs.append(out)
        return jnp.concatenate(outs, axis=1)

    def up_one(u_ref, col):
        u = jnp.dot(h, wup_ref[:, col:col + FF_CHUNK], preferred_element_type=F32)
        for s in range(n_slabs):
            u_ref[s, u_rows, :] = u[:, s * LANES:(s + 1) * LANES]

    def up(c):
        up_one(u_refs[2 * (c % FF_SLOTS)], c * FF_CHUNK)
        up_one(u_refs[2 * (c % FF_SLOTS) + 1], D_FF + c * FF_CHUNK)

    n_chunks = D_FF // FF_CHUNK
    acc = jnp.zeros((tm, D_MODEL), F32)
    up(0)
    for c in range(n_chunks):
        if c + 1 < n_chunks:
            up(c + 1)
        ug_ref, uv_ref = u_refs[2 * (c % FF_SLOTS)], u_refs[2 * (c % FF_SLOTS) + 1]
        gc = c * FF_CHUNK
        act = (_silu(conv(ug_ref, gc)) * conv(uv_ref, D_FF + gc)).astype(BF16)
        acc = acc + jnp.dot(act, wdn_ref[gc:gc + FF_CHUNK, :], preferred_element_type=F32)
    y_ref[...] = x1 + gate_ffn * acc


def _block_ffn_call(x, o, mod4, layer, row0, w_o, nw, w_up, conv_w, conv_b, w_down, *,
                    seq_len, tm):
    g_n, l_n, _ = x.shape
    of_layer = lambda g, i: (layer, 0, 0)
    tiles_per_seq = seq_len // tm
    nh = l_n // HALO
    hb = tm // HALO
    const = lambda g, i: (0, 0)
    tile = lambda g, i: (g, i, 0)
    prev = lambda g, i: (g, jnp.maximum(i * hb - 1, 0), 0)
    nxt = lambda g, i: (g, jnp.minimum((i + 1) * hb, nh - 1), 0)
    tile_specs = [pl.BlockSpec((None, tm, D_MODEL), tile)]
    x_args, o_args = [x], [o]
    if tiles_per_seq > 1:
        tile_specs += [pl.BlockSpec((None, HALO, D_MODEL), prev),
                       pl.BlockSpec((None, HALO, D_MODEL), nxt)]
        x_args, o_args = [x] * 3, [o] * 3
    return pl.pallas_call(
        functools.partial(_block_ffn_kernel, tiles_per_seq=tiles_per_seq),
        grid=(g_n, l_n // tm),
        in_specs=tile_specs + tile_specs + [
            pl.BlockSpec((None, None, 1, 6 * D_MODEL), lambda g, i: (layer, row0 + g, 0, 0)),
            pl.BlockSpec(w_o.shape, const, pipeline_mode=RESIDENT),
            pl.BlockSpec((1, D_MODEL), const),
            pl.BlockSpec((None,) + w_up.shape[1:], of_layer, pipeline_mode=RESIDENT),
            pl.BlockSpec((None,) + conv_w.shape[1:], of_layer),
            pl.BlockSpec((None,) + conv_b.shape[1:], of_layer),
            pl.BlockSpec((None,) + w_down.shape[1:], of_layer, pipeline_mode=RESIDENT),
        ],
        out_specs=pl.BlockSpec((None, tm, D_MODEL), tile),
        out_shape=jax.ShapeDtypeStruct(x.shape, F32),
        scratch_shapes=[pltpu.VMEM((FF_CHUNK // LANES, tm + 2 * HALO, LANES), F32)]
        * (2 * FF_SLOTS),
        compiler_params=_cparams(2),
        name="proj_conv_ffn",
    )(*x_args, *o_args, mod4, w_o, nw, w_up, conv_w, conv_b, w_down)


def _rope_tables(l_n):
    half = HEAD_DIM // 4
    freqs = ROPE_BASE ** (-jnp.arange(half, dtype=F32) / half)
    t = jnp.arange(l_n)
    lane = jnp.arange(HEAD_DIM)
    pos = jnp.where(lane[None, :] < HEAD_DIM // 2, (t // GRID_W)[:, None], (t % GRID_W)[:, None])
    ang = pos.astype(F32) * freqs[lane % half][None, :]
    sign = jnp.where((lane % (2 * half)) < half, -1.0, 1.0).astype(F32)
    cos = jnp.cos(ang)
    sin = jnp.sin(ang) * sign[None, :]
    return jnp.tile(cos, (1, 2)), jnp.tile(sin, (1, 2))


def _nbr_bias_kernel(rpb_ref, o_ref, *, patterns):
    n_dr = 2 * WIN_R - 1
    n_dc = 2 * WIN_C - 1
    base = pl.program_id(0) * (n_dr * n_dc)
    qc = lax.broadcasted_iota(jnp.int32, (GRID_W, LANES), 0)
    lane = lax.broadcasted_iota(jnp.int32, (GRID_W, LANES), 1)
    kc = lane % GRID_W
    dc = jnp.clip(kc - qc, -(WIN_C - 1), WIN_C - 1) + WIN_C - 1
    cs = jnp.clip(qc - WIN_C // 2, 0, GRID_W - WIN_C)
    col_ok = (kc >= cs) & (kc < cs + WIN_C)
    masked = jnp.full((GRID_W, LANES), NEG, F32)
    hit = [col_ok & (dc == d) for d in range(n_dc)]
    blocks = []
    for a in range(n_dr):
        t = masked
        for d in range(n_dc):
            t = jnp.where(hit[d], rpb_ref[base + a * n_dc + d], t)
        blocks.append(t)
    pick = lambda a: masked if a is None else blocks[a]
    lo = lane < GRID_W
    for cls, pat in enumerate(patterns):
        for i in range(NBR_ROWS):
            for m in range(NBR_WIN // 2):
                o_ref[cls, i * GRID_W:(i + 1) * GRID_W, m * LANES:(m + 1) * LANES] = jnp.where(
                    lo, pick(pat[i][2 * m]), pick(pat[i][2 * m + 1]))


def _nbr_bias_table(rpb, rows):
    n_heads = rpb.shape[0]
    patterns = tuple(_nbr_pattern(r0, rows) for r0 in (0, NBR_ROWS, rows - NBR_ROWS))
    mq = NBR_ROWS * GRID_W
    nk = NBR_WIN * GRID_W
    return pl.pallas_call(
        functools.partial(_nbr_bias_kernel, patterns=patterns),
        grid=(n_heads,),
        in_specs=[pl.BlockSpec(memory_space=pltpu.SMEM)],
        out_specs=pl.BlockSpec((len(patterns), None, mq, nk), lambda h: (0, h, 0, 0)),
        out_shape=jax.ShapeDtypeStruct((len(patterns), n_heads, mq, nk), F32),
        compiler_params=_cparams(1),
        name="nbr_bias",
    )(rpb.astype(F32).reshape(-1))


def _block_diag_mean():
    r = jnp.arange(NORM_CHUNK) // HEAD_DIM
    return jnp.where(r[:, None] == r[None, :], 1.0 / HEAD_DIM, 0.0).astype(BF16)


def kernel(x_prompt, x_sample, cache_k_a, cache_v_a, cache_k_b, cache_v_b, c, c_ctx,
           norm_attn_w, norm_ffn_w, w_ada, b_ada,
           w_qkv_a, q_norm_a, k_norm_a, sink_a, w_o_a,
           w_qkv_b, q_norm_b, k_norm_b, rpb_b, w_o_b,
           w_up, conv_w, conv_b, w_down):
    depth = w_ada.shape[0]
    batch, seq, _ = x_prompt.shape
    dec_batch, dec_seq, _ = x_sample.shape
    nq = N_HEADS * HEAD_DIM

    cond = jnp.concatenate(
        [c_ctx[None, :], c, jnp.zeros((N_MOD_ROWS - 1 - dec_batch, D_MODEL), F32)], axis=0)
    mod = _ada_call(cond.T, 1 + dec_batch, w_ada, b_ada[:, None, :])
    mod4 = mod.reshape(depth, N_MOD_ROWS, 1, 6 * D_MODEL)

    bd = _block_diag_mean()
    rope_tabs = _rope_tables(dec_seq)
    w_up_bf = w_up.astype(BF16)
    w_down_bf = w_down.astype(BF16)
    tile_w = lambda w, n: jnp.tile(w, n)[None, :]

    xp = x_prompt.reshape(1, batch * seq, D_MODEL)
    xs = x_sample
    new_k_a, new_v_a, new_k_b, new_v_b = [], [], [], []
    for i in range(depth):
        j = i // 2
        nw_attn = norm_attn_w[i][None, :]
        nw_ffn = norm_ffn_w[i][None, :]
        if i % 2 == 0:
            nkv = N_KV_A * HEAD_DIM
            w_qkv = w_qkv_a[j].astype(BF16)
            w_o = w_o_a[j].astype(BF16)
            qn, kn = tile_w(q_norm_a[j], N_HEADS), tile_w(k_norm_a[j], N_KV_A)
            q, k, v, kd, vd = _qkv_call(xp, mod4, i, 0, nw_attn, w_qkv, qn, kn, bd, None,
                                        nkv=nkv, dup=True, f32_kv=True, tm=512)
            new_k_a.append(k.reshape(batch, seq, N_KV_A, HEAD_DIM))
            new_v_a.append(v.reshape(batch, seq, N_KV_A, HEAD_DIM))
            shp = (batch, seq, -1)
            op = _ctx_attn_call(q.reshape(shp), kd.reshape(shp), vd.reshape(shp), sink_a[j])
            q, kd, vd = _qkv_call(xs, mod4, i, 1, nw_attn, w_qkv, qn, kn, bd, rope_tabs,
                                  nkv=nkv, dup=True, f32_kv=False, tm=512)
            o_s = _win_attn_call(q, kd, vd,
                                 cache_k_a[:, j].reshape(dec_batch, -1, nkv),
                                 cache_v_a[:, j].reshape(dec_batch, -1, nkv), sink_a[j])
        else:
            nkv = nq
            w_qkv = w_qkv_b[j].astype(BF16)
            w_o = w_o_b[j].astype(BF16)
            qn, kn = tile_w(q_norm_b[j], N_HEADS), tile_w(k_norm_b[j], N_HEADS)
            q, k, v = _qkv_call(xp, mod4, i, 0, nw_attn, w_qkv, qn, kn, bd, None,
                                nkv=nkv, dup=False, f32_kv=True, tm=512)
            new_k_b.append(k.reshape(batch, seq, N_HEADS, HEAD_DIM))
            new_v_b.append(v.reshape(batch, seq, N_HEADS, HEAD_DIM))
            shp = (batch, seq, -1)
            op = _ctx_attn_call(q.reshape(shp), k.reshape(shp), v.reshape(shp), None)
            q, k, v = _qkv_call(xs, mod4, i, 1, nw_attn, w_qkv, qn, kn, bd, None,
                                nkv=nkv, dup=False, f32_kv=False, tm=512)
            o_s = _nbr_attn_call(q, k, v, _nbr_bias_table(rpb_b[j], dec_seq // GRID_W),
                                 cache_k_b[:, j].reshape(dec_batch, -1, nq).astype(BF16),
                                 cache_v_b[:, j].reshape(dec_batch, -1, nq).astype(BF16))
        blk_w = (w_o, nw_ffn, w_up_bf, conv_w, conv_b[:, None, :], w_down_bf)
        xp = _block_ffn_call(xp, op.reshape(xp.shape), mod4, i, 0, *blk_w, seq_len=seq, tm=seq)
        xs = _block_ffn_call(xs, o_s, mod4, i, 1, *blk_w, seq_len=dec_seq, tm=512)
    return (xp.reshape(batch, seq, D_MODEL), xs,
            jnp.stack(new_k_a, axis=1), jnp.stack(new_v_a, axis=1),
            jnp.stack(new_k_b, axis=1), jnp.stack(new_v_b, axis=1))
```

```python
import functools

import jax
import jax.numpy as jnp
from jax import lax
from jax.experimental import pallas as pl
from jax.experimental.pallas import tpu as pltpu

D_MODEL = 1024
HEAD_DIM = 64
N_HEADS = 16
N_KV_A = 4
GRID_W = 64
WINDOW = 128
WIN_R = 8
WIN_C = 16
D_FF = 2816
ROPE_BASE = 10000.0
EPS = 1e-6
SCALE = HEAD_DIM ** -0.5

LANES = 128
PAD = 8
O_HALO = 16
NBR_ROWS = 4
NBR_WIN = NBR_ROWS + WIN_R
NORM_CHUNK = 256
FF_CHUNK = 256
FF_SLOTS = 3
NEG = -1e30
N_MOD_ROWS = 8

RESIDENT = pl.Buffered(1)

F32 = jnp.float32
BF16 = jnp.bfloat16
VMEM_LIMIT = 56 * 1024 * 1024


def _cparams(n_axes):
    return pltpu.CompilerParams(
        dimension_semantics=("arbitrary",) * n_axes, vmem_limit_bytes=VMEM_LIMIT)


def _silu(x):
    return x * (1.0 / (1.0 + jnp.exp(-x)))


def _rms_modulate(x, nw, shift, scale):
    ms = jnp.mean(x * x, axis=-1, keepdims=True)
    y = x * lax.rsqrt(ms + EPS) * nw
    return y * (1.0 + scale) + shift


def _ada_kernel(ct_ref, w_ref, b_ref, o_ref, *, n_cond):
    a = _silu(ct_ref[...])
    w = w_ref[...]
    rows = [jnp.sum(a[:, r:r + 1] * w, axis=0, keepdims=True) for r in range(n_cond)]
    rows.append(jnp.zeros((N_MOD_ROWS - n_cond, w.shape[1]), F32))
    o_ref[...] = jnp.concatenate(rows, axis=0) + b_ref[...]


def _ada_call(cond_t, n_cond, w_ada, b_ada):
    depth = w_ada.shape[0]
    tn = 2048
    return pl.pallas_call(
        functools.partial(_ada_kernel, n_cond=n_cond),
        grid=(depth, 6 * D_MODEL // tn),
        in_specs=[
            pl.BlockSpec((D_MODEL, N_MOD_ROWS), lambda l, n: (0, 0)),
            pl.BlockSpec((None, D_MODEL, tn), lambda l, n: (l, 0, n)),
            pl.BlockSpec((None, 1, tn), lambda l, n: (l, 0, n)),
        ],
        out_specs=pl.BlockSpec((None, N_MOD_ROWS, tn), lambda l, n: (l, 0, n)),
        out_shape=jax.ShapeDtypeStruct((depth, N_MOD_ROWS, 6 * D_MODEL), F32),
        compiler_params=_cparams(2),
        name="adaln",
    )(cond_t, w_ada, b_ada)


def _head_rms(t, bd, w):
    outs = []
    for c in range(t.shape[1] // NORM_CHUNK):
        tc = t[:, c * NORM_CHUNK:(c + 1) * NORM_CHUNK]
        ms = jnp.dot((tc * tc).astype(BF16), bd, preferred_element_type=F32)
        outs.append(tc * lax.rsqrt(ms + EPS))
    return jnp.concatenate(outs, axis=1) * w


def _rope(t, cos, sin_signed, lo16):
    outs = []
    for c in range(t.shape[1] // LANES):
        tc = t[:, c * LANES:(c + 1) * LANES]
        partner = jnp.where(lo16, pltpu.roll(tc, LANES - 16, axis=1), pltpu.roll(tc, 16, axis=1))
        outs.append(tc * cos + partner * sin_signed)
    return jnp.concatenate(outs, axis=1)


def _dup_heads(t, lo64):
    outs = []
    for c in range(t.shape[1] // LANES):
        tc = t[:, c * LANES:(c + 1) * LANES]
        sw = pltpu.roll(tc, HEAD_DIM, axis=1)
        outs.append(jnp.where(lo64, tc, sw))
        outs.append(jnp.where(lo64, sw, tc))
    return jnp.concatenate(outs, axis=1)


def _qkv_kernel(*refs, nkv, rope, dup, f32_kv):
    x_ref, mod_ref, nw_ref, w_ref, qn_ref, kn_ref, bd_ref = refs[:7]
    pos = 7
    if rope:
        cos_ref, sin_ref = refs[pos:pos + 2]
        pos += 2
    outs = refs[pos:]
    nq = N_HEADS * HEAD_DIM
    lane = lax.broadcasted_iota(jnp.int32, (1, LANES), 1)

    h = _rms_modulate(x_ref[...], nw_ref[...], mod_ref[:, 0:D_MODEL],
                      mod_ref[:, D_MODEL:2 * D_MODEL])
    qkv = jnp.dot(h.astype(BF16), w_ref[...], preferred_element_type=F32)
    bd = bd_ref[...]
    q = _head_rms(qkv[:, :nq], bd, qn_ref[...])
    k = _head_rms(qkv[:, nq:nq + nkv], bd, kn_ref[...])
    v = qkv[:, nq + nkv:]
    if rope:
        lo16 = (lane & 16) == 0
        q = _rope(q, cos_ref[...], sin_ref[...], lo16)
        k = _rope(k, cos_ref[...], sin_ref[...], lo16)
    outs[0][...] = (q * SCALE).astype(BF16)
    o = 1
    if f32_kv:
        outs[o][...] = k
        outs[o + 1][...] = v
        o += 2
    if dup:
        lo64 = lane < HEAD_DIM
        k = _dup_heads(k, lo64)
        v = _dup_heads(v, lo64)
    if dup or not f32_kv:
        outs[o][...] = k.astype(BF16)
        outs[o + 1][...] = v.astype(BF16)


def _qkv_call(x, mod4, layer, row0, nw, w, qn, kn, bd, rope_tabs, *, nkv, dup, f32_kv, tm):
    g_n, l_n, _ = x.shape
    rope = rope_tabs is not None
    nq = N_HEADS * HEAD_DIM
    xmap = lambda g, i: (g, i, 0)
    const = lambda g, i: (0, 0)
    in_specs = [
        pl.BlockSpec((None, tm, D_MODEL), xmap),
        pl.BlockSpec((None, None, 1, 6 * D_MODEL), lambda g, i: (layer, row0 + g, 0, 0)),
        pl.BlockSpec((1, D_MODEL), const),
        pl.BlockSpec(w.shape, const),
        pl.BlockSpec((1, nq), const),
        pl.BlockSpec((1, nkv), const),
        pl.BlockSpec((NORM_CHUNK, NORM_CHUNK), const),
    ]
    args = [x, mod4, nw, w, qn, kn, bd]
    if rope:
        in_specs += [pl.BlockSpec((tm, LANES), lambda g, i: (i, 0))] * 2
        args += list(rope_tabs)
    out_shape = [jax.ShapeDtypeStruct((g_n, l_n, nq), BF16)]
    out_specs = [pl.BlockSpec((None, tm, nq), xmap)]
    if f32_kv:
        out_shape += [jax.ShapeDtypeStruct((g_n, l_n, nkv), F32)] * 2
        out_specs += [pl.BlockSpec((None, tm, nkv), xmap)] * 2
    if dup or not f32_kv:
        wkv = nkv * 2 if dup else nkv
        out_shape += [jax.ShapeDtypeStruct((g_n, l_n, wkv), BF16)] * 2
        out_specs += [pl.BlockSpec((None, tm, wkv), xmap)] * 2
    return pl.pallas_call(
        functools.partial(_qkv_kernel, nkv=nkv, rope=rope, dup=dup, f32_kv=f32_kv),
        grid=(g_n, l_n // tm),
        in_specs=in_specs,
        out_specs=out_specs,
        out_shape=out_shape,
        compiler_params=_cparams(2),
        name="qkv",
    )(*args)


def _scores(q2_list, segs):
    lo = lax.broadcasted_iota(jnp.int32, (1, LANES), 1) < HEAD_DIM
    zero = jnp.zeros((), BF16)
    stacked = []
    for q2 in q2_list:
        stacked.append(jnp.where(lo, q2, zero))
        stacked.append(jnp.where(lo, zero, q2))
    qs = jnp.concatenate(stacked, axis=0)
    scores = []
    for k2, _, bias in segs:
        s = lax.dot_general(qs, k2, (((1,), (1,)), ((), ())), preferred_element_type=F32)
        scores.append(s if bias is None else s + bias)
    return scores


def _softmax_pv(scores, segs, sinks, m_rows):
    lo = lax.broadcasted_iota(jnp.int32, (1, LANES), 1) < HEAD_DIM
    lane_tiles = lambda t: [t[:, c * LANES:(c + 1) * LANES] for c in range(t.shape[1] // LANES)]
    mx_tile = functools.reduce(jnp.maximum, [t for s in scores for t in lane_tiles(s)])
    if sinks is not None:
        sink_tile = jnp.concatenate(
            [jnp.full((m_rows, LANES), sk, F32) for sk in sinks], axis=0)
        mx_tile = jnp.maximum(mx_tile, sink_tile)
    mx = jnp.max(mx_tile, axis=-1, keepdims=True)
    den_tile = (jnp.exp(sink_tile - mx) * (1.0 / LANES) if sinks is not None
                else jnp.zeros((scores[0].shape[0], LANES), F32))
    acc = jnp.zeros((scores[0].shape[0], LANES), F32)
    for s, (_, v2, _) in zip(scores, segs):
        p = jnp.exp(s - mx)
        den_tile = functools.reduce(jnp.add, lane_tiles(p), den_tile)
        acc = acc + jnp.dot(p.astype(BF16), v2, preferred_element_type=F32)
    rows = lambda t, n: t[n * m_rows:(n + 1) * m_rows]
    out = acc / jnp.sum(den_tile, axis=-1, keepdims=True)
    return [jnp.where(lo, rows(out, 2 * i), rows(out, 2 * i + 1))
            for i in range(out.shape[0] // (2 * m_rows))]


def _run_attention(jobs, lookahead=True):
    scores = _scores(*jobs[0][:2])
    for n, (q2_list, segs, sinks, store) in enumerate(jobs):
        more = n + 1 < len(jobs)
        if lookahead and more:
            nxt = _scores(*jobs[n + 1][:2])
        store(_softmax_pv(scores, segs, sinks, q2_list[0].shape[0]))
        if more:
            scores = nxt if lookahead else _scores(*jobs[n + 1][:2])


def _pair_slices():
    return [slice(j * LANES, (j + 1) * LANES) for j in range(N_HEADS // 2)]


def _pair_store(o_ref, sl):
    def _store(outs):
        o_ref[:, sl] = outs[0].astype(o_ref.dtype)
    return _store


def _dup_f32_to_bf16(t):
    lo64 = lax.broadcasted_iota(jnp.int32, (1, LANES), 1) < HEAD_DIM
    return _dup_heads(t, lo64).astype(BF16)


def _gqa_heads(q, segs_of_group, sink_ref, o_ref, lookahead):
    pairs_per_job = 2
    def store(pairs):
        def _store(outs):
            for p, out in zip(pairs, outs):
                o_ref[:, p * LANES:(p + 1) * LANES] = out.astype(o_ref.dtype)
        return _store

    jobs = []
    for g in range(N_KV_A):
        for first in range(2 * g, 2 * g + 2, pairs_per_job):
            pairs = list(range(first, first + pairs_per_job))
            jobs.append(([q[:, p * LANES:(p + 1) * LANES] for p in pairs], segs_of_group(g),
                         [sink_ref[2 * p + i] for p in pairs for i in range(2)], store(pairs)))
    _run_attention(jobs, lookahead)


def _ctx_attn_a_kernel(sink_ref, q_ref, k_ref, v_ref, o_ref):
    q = q_ref[...]
    k = k_ref[...]
    v = v_ref[...]
    tile = lambda t, g: t[:, g * LANES:(g + 1) * LANES]
    _gqa_heads(q, lambda g: [(tile(k, g), tile(v, g), None)], sink_ref, o_ref, True)


def _ctx_attn_b_kernel(q_ref, k_ref, v_ref, o_ref):
    q = q_ref[...]
    k = k_ref[...].astype(BF16)
    v = v_ref[...].astype(BF16)
    _run_attention([
        ([q[:, sl]], [(k[:, sl], v[:, sl], None)], None, _pair_store(o_ref, sl))
        for sl in _pair_slices()])


def _ctx_attn_call(q, k, v, sink):
    b_n, l_n, nq = q.shape
    bmap = lambda b: (b, 0, 0)
    in_specs = [pl.BlockSpec((None, l_n, nq), bmap),
                pl.BlockSpec((None, l_n, k.shape[2]), bmap),
                pl.BlockSpec((None, l_n, v.shape[2]), bmap)]
    args = [q, k, v]
    kern = _ctx_attn_b_kernel
    if sink is not None:
        in_specs = [pl.BlockSpec(memory_space=pltpu.SMEM)] + in_specs
        args = [sink] + args
        kern = _ctx_attn_a_kernel
    return pl.pallas_call(
        kern,
        grid=(b_n,),
        in_specs=in_specs,
        out_specs=pl.BlockSpec((None, l_n, nq), bmap),
        out_shape=jax.ShapeDtypeStruct((b_n, l_n, nq), BF16),
        compiler_params=_cparams(1),
        name="ctx_attn",
    )(*args)


def _win_attn_kernel(sink_ref, q_ref, kp_ref, kc_ref, kn_ref, vp_ref, vc_ref, vn_ref,
                     ck_ref, cv_ref, o_ref, kctx_ref, vctx_ref, bias_ref):
    i = pl.program_id(1)
    nb = pl.num_programs(1)
    blk = q_ref.shape[0]

    @pl.when(i == 0)
    def _():
        kctx_ref[...] = _dup_f32_to_bf16(ck_ref[...])
        vctx_ref[...] = _dup_f32_to_bf16(cv_ref[...])
        a = lax.broadcasted_iota(jnp.int32, bias_ref.shape[1:], 0) % blk
        j = lax.broadcasted_iota(jnp.int32, bias_ref.shape[1:], 1)
        band = (j >= a) & (j <= a + 2 * WINDOW)
        for cls, ok in enumerate((band & (j >= blk), band, band & (j < 2 * blk))):
            bias_ref[cls] = jnp.where(ok, 0.0, NEG).astype(F32)

    q = q_ref[...]
    k_loc = jnp.concatenate([kp_ref[...], kc_ref[...], kn_ref[...]], axis=0)
    v_loc = jnp.concatenate([vp_ref[...], vc_ref[...], vn_ref[...]], axis=0)
    bias = bias_ref[jnp.where(i == 0, 0, jnp.where(i == nb - 1, 2, 1))]
    tile = lambda t, g: t[:, g * LANES:(g + 1) * LANES]
    _gqa_heads(q, lambda g: [(tile(k_loc, g), tile(v_loc, g), bias),
                             (kctx_ref[:, g * LANES:(g + 1) * LANES],
                              vctx_ref[:, g * LANES:(g + 1) * LANES], None)],
               sink_ref, o_ref, True)


def _win_attn_call(q, kd, vd, ck, cv, sink):
    b_n, l_n, nq = q.shape
    blk = WINDOW
    nb = l_n // blk
    assert nb >= 3
    wkv = kd.shape[2]
    prev = lambda b, i: (b, jnp.maximum(i - 1, 0), 0)
    cur = lambda b, i: (b, i, 0)
    nxt = lambda b, i: (b, jnp.minimum(i + 1, nb - 1), 0)
    kv_specs = [pl.BlockSpec((None, blk, wkv), m) for m in (prev, cur, nxt)]
    ctx_spec = pl.BlockSpec((None,) + ck.shape[1:], lambda b, i: (b, 0, 0))
    return pl.pallas_call(
        _win_attn_kernel,
        grid=(b_n, nb),
        in_specs=[pl.BlockSpec(memory_space=pltpu.SMEM),
                  pl.BlockSpec((None, blk, nq), cur)] + kv_specs + kv_specs + [ctx_spec, ctx_spec],
        out_specs=pl.BlockSpec((None, blk, nq), cur),
        out_shape=jax.ShapeDtypeStruct((b_n, l_n, nq), BF16),
        scratch_shapes=[pltpu.VMEM((ck.shape[1], wkv), BF16)] * 2
        + [pltpu.VMEM((3, N_HEADS // N_KV_A * blk, 3 * blk), F32)],
        compiler_params=_cparams(2),
        name="win_attn",
    )(sink, q, kd, kd, kd, vd, vd, vd, ck, cv)


def _nbr_window_start(r0, rows):
    return jnp.clip(r0 - WIN_R // 2, 0, rows - NBR_WIN)


def _nbr_pattern(r0, rows):
    ws = min(max(r0 - WIN_R // 2, 0), rows - NBR_WIN)
    pat = []
    for i in range(NBR_ROWS):
        r = r0 + i
        rs = min(max(r - WIN_R // 2, 0), rows - WIN_R)
        pat.append(tuple((ws + w - r + WIN_R - 1) if rs <= ws + w < rs + WIN_R else None
                         for w in range(NBR_WIN)))
    return tuple(pat)


def _nbr_class(rb, n_rb):
    return jnp.where(rb == 0, 0, jnp.where(rb == n_rb - 1, 2, 1))


def _nbr_attn_kernel(q_ref, k_ref, v_ref, bias_ref, kc_ref, vc_ref, o_ref):
    q = q_ref[...]
    _run_attention([
        ([q[:, sl]],
         [(k_ref[0, :, sl], v_ref[0, :, sl],
           jnp.concatenate([bias_ref[2 * j], bias_ref[2 * j + 1]], axis=0)),
          (kc_ref[:, sl], vc_ref[:, sl], None)],
         None, _pair_store(o_ref, sl))
        for j, sl in enumerate(_pair_slices())])


def _nbr_attn_call(q, k, v, bias_tab, ck, cv):
    b_n, l_n, nq = q.shape
    rows = l_n // GRID_W
    n_rb = rows // NBR_ROWS
    interior = _nbr_pattern(NBR_ROWS, rows)
    assert all(_nbr_pattern(rb * NBR_ROWS, rows) == interior for rb in range(1, n_rb - 1))
    mq = NBR_ROWS * GRID_W
    nk = NBR_WIN * GRID_W
    blk = lambda b, rb: (b, rb, 0)
    ctx_spec = pl.BlockSpec((None,) + ck.shape[1:], lambda b, rb: (b, 0, 0))
    kv_spec = pl.BlockSpec(
        (pl.Element(1), pl.Element(nk), pl.Element(nq)),
        lambda b, rb: (b, _nbr_window_start(rb * NBR_ROWS, rows) * GRID_W, 0))
    bias_spec = pl.BlockSpec((None, N_HEADS, mq, nk),
                             lambda b, rb: (_nbr_class(rb, n_rb), 0, 0, 0))
    return pl.pallas_call(
        _nbr_attn_kernel,
        grid=(b_n, n_rb),
        in_specs=[pl.BlockSpec((None, mq, nq), blk), kv_spec, kv_spec, bias_spec,
                  ctx_spec, ctx_spec],
        out_specs=pl.BlockSpec((None, mq, nq), blk),
        out_shape=jax.ShapeDtypeStruct((b_n, l_n, nq), BF16),
        compiler_params=_cparams(2),
        name="nbr_attn",
    )(q, k, v, bias_tab, ck, cv)


def _block_ffn_kernel(*refs, tiles_per_seq):
    halo = tiles_per_seq > 1
    if halo:
        x_ref, xp_ref, xn_ref, o_ref, op_ref, on_ref = refs[:6]
        refs = refs[6:]
    else:
        x_ref, o_ref = refs[:2]
        refs = refs[2:]
    mod_ref, wo_ref, nw_ref, wup_ref, cw_ref, cb_ref, wdn_ref, y_ref = refs[:8]
    u_refs = refs[8:]
    tm = x_ref.shape[0]
    n_slabs = FF_CHUNK // LANES
    gate_attn = mod_ref[:, 2 * D_MODEL:3 * D_MODEL]
    shift = mod_ref[:, 3 * D_MODEL:4 * D_MODEL]
    scale = mod_ref[:, 4 * D_MODEL:5 * D_MODEL]
    gate_ffn = mod_ref[:, 5 * D_MODEL:6 * D_MODEL]
    if halo:
        i = pl.program_id(1) % tiles_per_seq
        x_ext = jnp.concatenate([x_ref[...], xn_ref[...], xp_ref[...]], axis=0)
        o_nbr = jnp.concatenate([on_ref[...].astype(F32)[:PAD],
                                 op_ref[...].astype(F32)[O_HALO - PAD:]], axis=0)
        o_ext = jnp.concatenate([o_ref[...], o_nbr.astype(BF16)], axis=0)
        x1_ext = x_ext + gate_attn * jnp.dot(o_ext, wo_ref[...], preferred_element_type=F32)
        x1 = x1_ext[:tm]
        row = lax.broadcasted_iota(jnp.int32, (tm + 2 * PAD, 1), 0)
        inside = ((row < tm) | ((row < tm + PAD) & (i < tiles_per_seq - 1))
                  | ((row >= tm + PAD) & (i > 0)))
        h = jnp.where(inside, _rms_modulate(x1_ext, nw_ref[...], shift, scale), 0.0).astype(BF16)
        placement = ((slice(0, tm), pl.ds(PAD, tm)), (slice(tm, tm + PAD), pl.ds(PAD + tm, PAD)),
                     (slice(tm + PAD, tm + 2 * PAD), pl.ds(0, PAD)))
    else:
        x1 = x_ref[...] + gate_attn * jnp.dot(o_ref[...], wo_ref[...],
                                              preferred_element_type=F32)
        h = _rms_modulate(x1, nw_ref[...], shift, scale).astype(BF16)
        placement = ((slice(0, tm), pl.ds(PAD, tm)),)
        pad = jnp.zeros((PAD, LANES), F32)
        for u_ref in u_refs:
            for s in range(n_slabs):
                u_ref[s, pl.ds(0, PAD), :] = pad
                u_ref[s, pl.ds(PAD + tm, PAD), :] = pad

    def conv(u_ref, col):
        outs = []
        for s in range(n_slabs):
            lanes = slice(col + s * LANES, col + (s + 1) * LANES)
            cw = cw_ref[:, lanes]
            out = cb_ref[:, lanes]
            for o in range(3):
                out = out + u_ref[s, pl.ds(PAD - 1 + o, tm), :] * cw[o:o + 1, :]
            outs.append(out)
        return jnp.concatenate(outs, axis=1)

    def up_one(u_ref, col):
        u = jnp.dot(h, wup_ref[:, col:col + FF_CHUNK], preferred_element_type=F32)
        for s in range(n_slabs):
            for src, dst in placement:
                u_ref[s, dst, :] = u[src, s * LANES:(s + 1) * LANES]

    def up(c):
        up_one(u_refs[2 * (c % FF_SLOTS)], c * FF_CHUNK)
        up_one(u_refs[2 * (c % FF_SLOTS) + 1], D_FF + c * FF_CHUNK)

    n_chunks = D_FF // FF_CHUNK
    acc = jnp.zeros((tm, D_MODEL), F32)
    up(0)
    for c in range(n_chunks):
        if c + 1 < n_chunks:
            up(c + 1)
        ug_ref, uv_ref = u_refs[2 * (c % FF_SLOTS)], u_refs[2 * (c % FF_SLOTS) + 1]
        gc = c * FF_CHUNK
        act = (_silu(conv(ug_ref, gc)) * conv(uv_ref, D_FF + gc)).astype(BF16)
        acc = acc + jnp.dot(act, wdn_ref[gc:gc + FF_CHUNK, :], preferred_element_type=F32)
    y_ref[...] = x1 + gate_ffn * acc


def _block_ffn_call(x, o, mod4, layer, row0, w_o, nw, w_up, conv_w, conv_b, w_down, *,
                    seq_len, tm):
    g_n, l_n, _ = x.shape
    of_layer = lambda g, i: (layer, 0, 0)
    tiles_per_seq = seq_len // tm
    const = lambda g, i: (0, 0)
    tile = lambda g, i: (g, i, 0)

    def with_neighbours(rows):
        specs = [pl.BlockSpec((None, tm, D_MODEL), tile)]
        if tiles_per_seq > 1:
            per_tile, last = tm // rows, l_n // rows - 1
            specs += [
                pl.BlockSpec((None, rows, D_MODEL),
                             lambda g, i: (g, jnp.maximum(i * per_tile - 1, 0), 0)),
                pl.BlockSpec((None, rows, D_MODEL),
                             lambda g, i: (g, jnp.minimum((i + 1) * per_tile, last), 0))]
        return specs

    x_args, o_args = ([x] * 3, [o] * 3) if tiles_per_seq > 1 else ([x], [o])
    return pl.pallas_call(
        functools.partial(_block_ffn_kernel, tiles_per_seq=tiles_per_seq),
        grid=(g_n, l_n // tm),
        in_specs=with_neighbours(PAD) + with_neighbours(O_HALO) + [
            pl.BlockSpec((None, None, 1, 6 * D_MODEL), lambda g, i: (layer, row0 + g, 0, 0)),
            pl.BlockSpec(w_o.shape, const, pipeline_mode=RESIDENT),
            pl.BlockSpec((1, D_MODEL), const),
            pl.BlockSpec((None,) + w_up.shape[1:], of_layer, pipeline_mode=RESIDENT),
            pl.BlockSpec((None,) + conv_w.shape[1:], of_layer),
            pl.BlockSpec((None,) + conv_b.shape[1:], of_layer),
            pl.BlockSpec((None,) + w_down.shape[1:], of_layer, pipeline_mode=RESIDENT),
        ],
        out_specs=pl.BlockSpec((None, tm, D_MODEL), tile),
        out_shape=jax.ShapeDtypeStruct(x.shape, F32),
        scratch_shapes=[pltpu.VMEM((FF_CHUNK // LANES, tm + 2 * PAD, LANES), F32)]
        * (2 * FF_SLOTS),
        compiler_params=_cparams(2),
        name="proj_conv_ffn",
    )(*x_args, *o_args, mod4, w_o, nw, w_up, conv_w, conv_b, w_down)


def _rope_tables(l_n):
    half = HEAD_DIM // 4
    freqs = ROPE_BASE ** (-jnp.arange(half, dtype=F32) / half)
    t = jnp.arange(l_n)
    lane = jnp.arange(HEAD_DIM)
    pos = jnp.where(lane[None, :] < HEAD_DIM // 2, (t // GRID_W)[:, None], (t % GRID_W)[:, None])
    ang = pos.astype(F32) * freqs[lane % half][None, :]
    sign = jnp.where((lane % (2 * half)) < half, -1.0, 1.0).astype(F32)
    cos = jnp.cos(ang)
    sin = jnp.sin(ang) * sign[None, :]
    return jnp.tile(cos, (1, 2)), jnp.tile(sin, (1, 2))


def _nbr_bias_kernel(rpb_ref, o_ref, *, patterns):
    n_dr = 2 * WIN_R - 1
    n_dc = 2 * WIN_C - 1
    base = pl.program_id(0) * (n_dr * n_dc)
    qc = lax.broadcasted_iota(jnp.int32, (GRID_W, LANES), 0)
    lane = lax.broadcasted_iota(jnp.int32, (GRID_W, LANES), 1)
    kc = lane % GRID_W
    dc = jnp.clip(kc - qc, -(WIN_C - 1), WIN_C - 1) + WIN_C - 1
    cs = jnp.clip(qc - WIN_C // 2, 0, GRID_W - WIN_C)
    col_ok = (kc >= cs) & (kc < cs + WIN_C)
    masked = jnp.full((GRID_W, LANES), NEG, F32)
    hit = [col_ok & (dc == d) for d in range(n_dc)]
    blocks = []
    for a in range(n_dr):
        t = masked
        for d in range(n_dc):
            t = jnp.where(hit[d], rpb_ref[base + a * n_dc + d], t)
        blocks.append(t)
    pick = lambda a: masked if a is None else blocks[a]
    lo = lane < GRID_W
    for cls, pat in enumerate(patterns):
        for i in range(NBR_ROWS):
            for m in range(NBR_WIN // 2):
                o_ref[cls, i * GRID_W:(i + 1) * GRID_W, m * LANES:(m + 1) * LANES] = jnp.where(
                    lo, pick(pat[i][2 * m]), pick(pat[i][2 * m + 1]))


def _nbr_bias_table(rpb, rows):
    n_heads = rpb.shape[0]
    patterns = tuple(_nbr_pattern(r0, rows) for r0 in (0, NBR_ROWS, rows - NBR_ROWS))
    mq = NBR_ROWS * GRID_W
    nk = NBR_WIN * GRID_W
    return pl.pallas_call(
        functools.partial(_nbr_bias_kernel, patterns=patterns),
        grid=(n_heads,),
        in_specs=[pl.BlockSpec(memory_space=pltpu.SMEM)],
        out_specs=pl.BlockSpec((len(patterns), None, mq, nk), lambda h: (0, h, 0, 0)),
        out_shape=jax.ShapeDtypeStruct((len(patterns), n_heads, mq, nk), F32),
        compiler_params=_cparams(1),
        name="nbr_bias",
    )(rpb.astype(F32).reshape(-1))


def _block_diag_mean():
    r = jnp.arange(NORM_CHUNK) // HEAD_DIM
    return jnp.where(r[:, None] == r[None, :], 1.0 / HEAD_DIM, 0.0).astype(BF16)


def kernel(x_prompt, x_sample, cache_k_a, cache_v_a, cache_k_b, cache_v_b, c, c_ctx,
           norm_attn_w, norm_ffn_w, w_ada, b_ada,
           w_qkv_a, q_norm_a, k_norm_a, sink_a, w_o_a,
           w_qkv_b, q_norm_b, k_norm_b, rpb_b, w_o_b,
           w_up, conv_w, conv_b, w_down):
    depth = w_ada.shape[0]
    batch, seq, _ = x_prompt.shape
    dec_batch, dec_seq, _ = x_sample.shape
    nq = N_HEADS * HEAD_DIM

    cond = jnp.concatenate(
        [c_ctx[None, :], c, jnp.zeros((N_MOD_ROWS - 1 - dec_batch, D_MODEL), F32)], axis=0)
    mod = _ada_call(cond.T, 1 + dec_batch, w_ada, b_ada[:, None, :])
    mod4 = mod.reshape(depth, N_MOD_ROWS, 1, 6 * D_MODEL)

    bd = _block_diag_mean()
    rope_tabs = _rope_tables(dec_seq)
    w_up_bf = w_up.astype(BF16)
    w_down_bf = w_down.astype(BF16)
    tile_w = lambda w, n: jnp.tile(w, n)[None, :]

    xp = x_prompt.reshape(1, batch * seq, D_MODEL)
    xs = x_sample
    new_k_a, new_v_a, new_k_b, new_v_b = [], [], [], []
    for i in range(depth):
        j = i // 2
        nw_attn = norm_attn_w[i][None, :]
        nw_ffn = norm_ffn_w[i][None, :]
        if i % 2 == 0:
            nkv = N_KV_A * HEAD_DIM
            w_qkv = w_qkv_a[j].astype(BF16)
            w_o = w_o_a[j].astype(BF16)
            qn, kn = tile_w(q_norm_a[j], N_HEADS), tile_w(k_norm_a[j], N_KV_A)
            q, k, v, kd, vd = _qkv_call(xp, mod4, i, 0, nw_attn, w_qkv, qn, kn, bd, None,
                                        nkv=nkv, dup=True, f32_kv=True, tm=512)
            new_k_a.append(k.reshape(batch, seq, N_KV_A, HEAD_DIM))
            new_v_a.append(v.reshape(batch, seq, N_KV_A, HEAD_DIM))
            shp = (batch, seq, -1)
            op = _ctx_attn_call(q.reshape(shp), kd.reshape(shp), vd.reshape(shp), sink_a[j])
            q, kd, vd = _qkv_call(xs, mod4, i, 1, nw_attn, w_qkv, qn, kn, bd, rope_tabs,
                                  nkv=nkv, dup=True, f32_kv=False, tm=512)
            o_s = _win_attn_call(q, kd, vd,
                                 cache_k_a[:, j].reshape(dec_batch, -1, nkv),
                                 cache_v_a[:, j].reshape(dec_batch, -1, nkv), sink_a[j])
        else:
            nkv = nq
            w_qkv = w_qkv_b[j].astype(BF16)
            w_o = w_o_b[j].astype(BF16)
            qn, kn = tile_w(q_norm_b[j], N_HEADS), tile_w(k_norm_b[j], N_HEADS)
            q, k, v = _qkv_call(xp, mod4, i, 0, nw_attn, w_qkv, qn, kn, bd, None,
                                nkv=nkv, dup=False, f32_kv=True, tm=512)
            new_k_b.append(k.reshape(batch, seq, N_HEADS, HEAD_DIM))
            new_v_b.append(v.reshape(batch, seq, N_HEADS, HEAD_DIM))
            shp = (batch, seq, -1)
            op = _ctx_attn_call(q.reshape(shp), k.reshape(shp), v.reshape(shp), None)
            q, k, v = _qkv_call(xs, mod4, i, 1, nw_attn, w_qkv, qn, kn, bd, None,
                                nkv=nkv, dup=False, f32_kv=False, tm=512)
            o_s = _nbr_attn_call(q, k, v, _nbr_bias_table(rpb_b[j], dec_seq // GRID_W),
                                 cache_k_b[:, j].reshape(dec_batch, -1, nq).astype(BF16),
                                 cache_v_b[:, j].reshape(dec_batch, -1, nq).astype(BF16))
        blk_w = (w_o, nw_ffn, w_up_bf, conv_w, conv_b[:, None, :], w_down_bf)
        xp = _block_ffn_call(xp, op.reshape(xp.shape), mod4, i, 0, *blk_w, seq_len=seq, tm=seq)
        xs = _block_ffn_call(xs, o_s, mod4, i, 1, *blk_w, seq_len=dec_seq, tm=512)
    return (xp.reshape(batch, seq, D_MODEL), xs,
            jnp.stack(new_k_a, axis=1), jnp.stack(new_v_a, axis=1),
            jnp.stack(new_k_b, axis=1), jnp.stack(new_v_b, axis=1))
```

```python
import functools

import jax
import jax.numpy as jnp
from jax import lax
from jax.experimental import pallas as pl
from jax.experimental.pallas import tpu as pltpu

D_MODEL = 1024
HEAD_DIM = 64
N_HEADS = 16
N_KV_A = 4
GRID_W = 64
WINDOW = 128
WIN_R = 8
WIN_C = 16
D_FF = 2816
ROPE_BASE = 10000.0
EPS = 1e-6
SCALE = HEAD_DIM ** -0.5

LANES = 128
PAD = 8
O_HALO = 16
NBR_ROWS = 4
NBR_WIN = NBR_ROWS + WIN_R
NORM_CHUNK = 256
FF_CHUNK = 256
FF_SLOTS = 3
NEG = -1e30
N_MOD_ROWS = 8

RESIDENT = pl.Buffered(1)

F32 = jnp.float32
BF16 = jnp.bfloat16
VMEM_LIMIT = 56 * 1024 * 1024


def _cparams(n_axes):
    return pltpu.CompilerParams(
        dimension_semantics=("arbitrary",) * n_axes, vmem_limit_bytes=VMEM_LIMIT)


def _silu(x):
    return x * (1.0 / (1.0 + jnp.exp(-x)))


def _rms_modulate(x, nw, shift, scale):
    ms = jnp.mean(x * x, axis=-1, keepdims=True)
    return x * lax.rsqrt(ms + EPS) * (nw * (1.0 + scale)) + shift


def _ada_kernel(ct_ref, w_ref, b_ref, o_ref, *, n_cond):
    a = _silu(ct_ref[...])
    w = w_ref[...]
    rows = [jnp.sum(a[:, r:r + 1] * w, axis=0, keepdims=True) for r in range(n_cond)]
    rows.append(jnp.zeros((N_MOD_ROWS - n_cond, w.shape[1]), F32))
    o_ref[...] = jnp.concatenate(rows, axis=0) + b_ref[...]


def _ada_call(cond_t, n_cond, w_ada, b_ada):
    depth = w_ada.shape[0]
    tn = 2048
    return pl.pallas_call(
        functools.partial(_ada_kernel, n_cond=n_cond),
        grid=(depth, 6 * D_MODEL // tn),
        in_specs=[
            pl.BlockSpec((D_MODEL, N_MOD_ROWS), lambda l, n: (0, 0)),
            pl.BlockSpec((None, D_MODEL, tn), lambda l, n: (l, 0, n)),
            pl.BlockSpec((None, 1, tn), lambda l, n: (l, 0, n)),
        ],
        out_specs=pl.BlockSpec((None, N_MOD_ROWS, tn), lambda l, n: (l, 0, n)),
        out_shape=jax.ShapeDtypeStruct((depth, N_MOD_ROWS, 6 * D_MODEL), F32),
        compiler_params=_cparams(2),
        name="adaln",
    )(cond_t, w_ada, b_ada)


def _head_rms(t, bd, w):
    outs = []
    for c in range(t.shape[1] // NORM_CHUNK):
        tc = t[:, c * NORM_CHUNK:(c + 1) * NORM_CHUNK]
        ms = jnp.dot((tc * tc).astype(BF16), bd, preferred_element_type=F32)
        outs.append(tc * lax.rsqrt(ms + EPS))
    return jnp.concatenate(outs, axis=1) * w


def _rope(t, cos, sin_signed, lo16):
    outs = []
    for c in range(t.shape[1] // LANES):
        tc = t[:, c * LANES:(c + 1) * LANES]
        partner = jnp.where(lo16, pltpu.roll(tc, LANES - 16, axis=1), pltpu.roll(tc, 16, axis=1))
        outs.append(tc * cos + partner * sin_signed)
    return jnp.concatenate(outs, axis=1)


def _dup_heads(t, lo64):
    outs = []
    for c in range(t.shape[1] // LANES):
        tc = t[:, c * LANES:(c + 1) * LANES]
        sw = pltpu.roll(tc, HEAD_DIM, axis=1)
        outs.append(jnp.where(lo64, tc, sw))
        outs.append(jnp.where(lo64, sw, tc))
    return jnp.concatenate(outs, axis=1)


def _qkv_kernel(*refs, nkv, rope, dup, f32_kv):
    x_ref, mod_ref, nw_ref, w_ref, qn_ref, kn_ref, bd_ref = refs[:7]
    pos = 7
    if rope:
        cos_ref, sin_ref = refs[pos:pos + 2]
        pos += 2
    outs = refs[pos:]
    nq = N_HEADS * HEAD_DIM
    lane = lax.broadcasted_iota(jnp.int32, (1, LANES), 1)

    h = _rms_modulate(x_ref[...], nw_ref[...], mod_ref[:, 0:D_MODEL],
                      mod_ref[:, D_MODEL:2 * D_MODEL])
    qkv = jnp.dot(h.astype(BF16), w_ref[...], preferred_element_type=F32)
    bd = bd_ref[...]
    q = _head_rms(qkv[:, :nq], bd, qn_ref[...])
    k = _head_rms(qkv[:, nq:nq + nkv], bd, kn_ref[...])
    v = qkv[:, nq + nkv:]
    if rope:
        lo16 = (lane & 16) == 0
        q = _rope(q, cos_ref[...], sin_ref[...], lo16)
        k = _rope(k, cos_ref[...], sin_ref[...], lo16)
    outs[0][...] = (q * SCALE).astype(BF16)
    o = 1
    if f32_kv:
        outs[o][...] = k
        outs[o + 1][...] = v
        o += 2
    if dup:
        lo64 = lane < HEAD_DIM
        k = _dup_heads(k, lo64)
        v = _dup_heads(v, lo64)
    if dup or not f32_kv:
        outs[o][...] = k.astype(BF16)
        outs[o + 1][...] = v.astype(BF16)


def _qkv_call(x, mod4, layer, row0, nw, w, qn, kn, bd, rope_tabs, *, nkv, dup, f32_kv, tm):
    g_n, l_n, _ = x.shape
    rope = rope_tabs is not None
    nq = N_HEADS * HEAD_DIM
    xmap = lambda g, i: (g, i, 0)
    const = lambda g, i: (0, 0)
    in_specs = [
        pl.BlockSpec((None, tm, D_MODEL), xmap),
        pl.BlockSpec((None, None, 1, 6 * D_MODEL), lambda g, i: (layer, row0 + g, 0, 0)),
        pl.BlockSpec((1, D_MODEL), const),
        pl.BlockSpec(w.shape, const),
        pl.BlockSpec((1, nq), const),
        pl.BlockSpec((1, nkv), const),
        pl.BlockSpec((NORM_CHUNK, NORM_CHUNK), const),
    ]
    args = [x, mod4, nw, w, qn, kn, bd]
    if rope:
        in_specs += [pl.BlockSpec((tm, LANES), lambda g, i: (i, 0))] * 2
        args += list(rope_tabs)
    out_shape = [jax.ShapeDtypeStruct((g_n, l_n, nq), BF16)]
    out_specs = [pl.BlockSpec((None, tm, nq), xmap)]
    if f32_kv:
        out_shape += [jax.ShapeDtypeStruct((g_n, l_n, nkv), F32)] * 2
        out_specs += [pl.BlockSpec((None, tm, nkv), xmap)] * 2
    if dup or not f32_kv:
        wkv = nkv * 2 if dup else nkv
        out_shape += [jax.ShapeDtypeStruct((g_n, l_n, wkv), BF16)] * 2
        out_specs += [pl.BlockSpec((None, tm, wkv), xmap)] * 2
    return pl.pallas_call(
        functools.partial(_qkv_kernel, nkv=nkv, rope=rope, dup=dup, f32_kv=f32_kv),
        grid=(g_n, l_n // tm),
        in_specs=in_specs,
        out_specs=out_specs,
        out_shape=out_shape,
        compiler_params=_cparams(2),
        name="qkv",
    )(*args)


def _scores(q2_list, segs):
    lo = lax.broadcasted_iota(jnp.int32, (1, LANES), 1) < HEAD_DIM
    zero = jnp.zeros((), BF16)
    stacked = []
    for q2 in q2_list:
        stacked.append(jnp.where(lo, q2, zero))
        stacked.append(jnp.where(lo, zero, q2))
    qs = jnp.concatenate(stacked, axis=0)
    scores = []
    for k2, _, bias in segs:
        s = lax.dot_general(qs, k2, (((1,), (1,)), ((), ())), preferred_element_type=F32)
        scores.append(s if bias is None else s + bias)
    return scores


def _softmax_pv(scores, segs, sinks, m_rows):
    lo = lax.broadcasted_iota(jnp.int32, (1, LANES), 1) < HEAD_DIM
    lane_tiles = lambda t: [t[:, c * LANES:(c + 1) * LANES] for c in range(t.shape[1] // LANES)]
    mx_tile = functools.reduce(jnp.maximum, [t for s in scores for t in lane_tiles(s)])
    if sinks is not None:
        sink_tile = jnp.concatenate(
            [jnp.full((m_rows, LANES), sk, F32) for sk in sinks], axis=0)
        mx_tile = jnp.maximum(mx_tile, sink_tile)
    mx = jnp.max(mx_tile, axis=-1, keepdims=True)
    den_tile = (jnp.exp(sink_tile - mx) * (1.0 / LANES) if sinks is not None
                else jnp.zeros((scores[0].shape[0], LANES), F32))
    acc = jnp.zeros((scores[0].shape[0], LANES), F32)
    for s, (_, v2, _) in zip(scores, segs):
        p = jnp.exp(s - mx)
        den_tile = functools.reduce(jnp.add, lane_tiles(p), den_tile)
        acc = acc + jnp.dot(p.astype(BF16), v2, preferred_element_type=F32)
    rows = lambda t, n: t[n * m_rows:(n + 1) * m_rows]
    out = acc / jnp.sum(den_tile, axis=-1, keepdims=True)
    return [jnp.where(lo, rows(out, 2 * i), rows(out, 2 * i + 1))
            for i in range(out.shape[0] // (2 * m_rows))]


def _run_attention(jobs, lookahead=True):
    scores = _scores(*jobs[0][:2])
    for n, (q2_list, segs, sinks, store) in enumerate(jobs):
        more = n + 1 < len(jobs)
        if lookahead and more:
            nxt = _scores(*jobs[n + 1][:2])
        store(_softmax_pv(scores, segs, sinks, q2_list[0].shape[0]))
        if more:
            scores = nxt if lookahead else _scores(*jobs[n + 1][:2])


def _pair_slices():
    return [slice(j * LANES, (j + 1) * LANES) for j in range(N_HEADS // 2)]


def _pair_store(o_ref, sl):
    def _store(outs):
        o_ref[:, sl] = outs[0].astype(o_ref.dtype)
    return _store


def _dup_f32_to_bf16(t):
    lo64 = lax.broadcasted_iota(jnp.int32, (1, LANES), 1) < HEAD_DIM
    return _dup_heads(t, lo64).astype(BF16)


def _gqa_heads(q, segs_of_group, sink_ref, o_ref, lookahead):
    pairs_per_job = 2
    def store(pairs):
        def _store(outs):
            for p, out in zip(pairs, outs):
                o_ref[:, p * LANES:(p + 1) * LANES] = out.astype(o_ref.dtype)
        return _store

    jobs = []
    for g in range(N_KV_A):
        for first in range(2 * g, 2 * g + 2, pairs_per_job):
            pairs = list(range(first, first + pairs_per_job))
            jobs.append(([q[:, p * LANES:(p + 1) * LANES] for p in pairs], segs_of_group(g),
                         [sink_ref[2 * p + i] for p in pairs for i in range(2)], store(pairs)))
    _run_attention(jobs, lookahead)


def _ctx_attn_a_kernel(sink_ref, q_ref, k_ref, v_ref, o_ref):
    q = q_ref[...]
    k = k_ref[...]
    v = v_ref[...]
    tile = lambda t, g: t[:, g * LANES:(g + 1) * LANES]
    _gqa_heads(q, lambda g: [(tile(k, g), tile(v, g), None)], sink_ref, o_ref, True)


def _ctx_attn_b_kernel(q_ref, k_ref, v_ref, o_ref):
    q = q_ref[...]
    k = k_ref[...].astype(BF16)
    v = v_ref[...].astype(BF16)
    _run_attention([
        ([q[:, sl]], [(k[:, sl], v[:, sl], None)], None, _pair_store(o_ref, sl))
        for sl in _pair_slices()])


def _ctx_attn_call(q, k, v, sink):
    b_n, l_n, nq = q.shape
    bmap = lambda b: (b, 0, 0)
    in_specs = [pl.BlockSpec((None, l_n, nq), bmap),
                pl.BlockSpec((None, l_n, k.shape[2]), bmap),
                pl.BlockSpec((None, l_n, v.shape[2]), bmap)]
    args = [q, k, v]
    kern = _ctx_attn_b_kernel
    if sink is not None:
        in_specs = [pl.BlockSpec(memory_space=pltpu.SMEM)] + in_specs
        args = [sink] + args
        kern = _ctx_attn_a_kernel
    return pl.pallas_call(
        kern,
        grid=(b_n,),
        in_specs=in_specs,
        out_specs=pl.BlockSpec((None, l_n, nq), bmap),
        out_shape=jax.ShapeDtypeStruct((b_n, l_n, nq), BF16),
        compiler_params=_cparams(1),
        name="ctx_attn",
    )(*args)


def _win_attn_kernel(sink_ref, q_ref, kp_ref, kc_ref, kn_ref, vp_ref, vc_ref, vn_ref,
                     ck_ref, cv_ref, o_ref, kctx_ref, vctx_ref, bias_ref):
    i = pl.program_id(1)
    nb = pl.num_programs(1)
    blk = q_ref.shape[0]

    @pl.when(i == 0)
    def _():
        kctx_ref[...] = _dup_f32_to_bf16(ck_ref[...])
        vctx_ref[...] = _dup_f32_to_bf16(cv_ref[...])
        a = lax.broadcasted_iota(jnp.int32, bias_ref.shape[1:], 0) % blk
        j = lax.broadcasted_iota(jnp.int32, bias_ref.shape[1:], 1)
        band = (j >= a) & (j <= a + 2 * WINDOW)
        for cls, ok in enumerate((band & (j >= blk), band, band & (j < 2 * blk))):
            bias_ref[cls] = jnp.where(ok, 0.0, NEG).astype(F32)

    q = q_ref[...]
    k_loc = jnp.concatenate([kp_ref[...], kc_ref[...], kn_ref[...]], axis=0)
    v_loc = jnp.concatenate([vp_ref[...], vc_ref[...], vn_ref[...]], axis=0)
    bias = bias_ref[jnp.where(i == 0, 0, jnp.where(i == nb - 1, 2, 1))]
    tile = lambda t, g: t[:, g * LANES:(g + 1) * LANES]
    _gqa_heads(q, lambda g: [(tile(k_loc, g), tile(v_loc, g), bias),
                             (kctx_ref[:, g * LANES:(g + 1) * LANES],
                              vctx_ref[:, g * LANES:(g + 1) * LANES], None)],
               sink_ref, o_ref, True)


def _win_attn_call(q, kd, vd, ck, cv, sink):
    b_n, l_n, nq = q.shape
    blk = WINDOW
    nb = l_n // blk
    assert nb >= 3
    wkv = kd.shape[2]
    prev = lambda b, i: (b, jnp.maximum(i - 1, 0), 0)
    cur = lambda b, i: (b, i, 0)
    nxt = lambda b, i: (b, jnp.minimum(i + 1, nb - 1), 0)
    kv_specs = [pl.BlockSpec((None, blk, wkv), m) for m in (prev, cur, nxt)]
    ctx_spec = pl.BlockSpec((None,) + ck.shape[1:], lambda b, i: (b, 0, 0))
    return pl.pallas_call(
        _win_attn_kernel,
        grid=(b_n, nb),
        in_specs=[pl.BlockSpec(memory_space=pltpu.SMEM),
                  pl.BlockSpec((None, blk, nq), cur)] + kv_specs + kv_specs + [ctx_spec, ctx_spec],
        out_specs=pl.BlockSpec((None, blk, nq), cur),
        out_shape=jax.ShapeDtypeStruct((b_n, l_n, nq), BF16),
        scratch_shapes=[pltpu.VMEM((ck.shape[1], wkv), BF16)] * 2
        + [pltpu.VMEM((3, N_HEADS // N_KV_A * blk, 3 * blk), F32)],
        compiler_params=_cparams(2),
        name="win_attn",
    )(sink, q, kd, kd, kd, vd, vd, vd, ck, cv)


def _nbr_window_start(r0, rows):
    return jnp.clip(r0 - WIN_R // 2, 0, rows - NBR_WIN)


def _nbr_pattern(r0, rows):
    ws = min(max(r0 - WIN_R // 2, 0), rows - NBR_WIN)
    pat = []
    for i in range(NBR_ROWS):
        r = r0 + i
        rs = min(max(r - WIN_R // 2, 0), rows - WIN_R)
        pat.append(tuple((ws + w - r + WIN_R - 1) if rs <= ws + w < rs + WIN_R else None
                         for w in range(NBR_WIN)))
    return tuple(pat)


def _nbr_class(rb, n_rb):
    return jnp.where(rb == 0, 0, jnp.where(rb == n_rb - 1, 2, 1))


def _nbr_attn_kernel(q_ref, k_ref, v_ref, bias_ref, kc_ref, vc_ref, o_ref):
    q = q_ref[...]
    _run_attention([
        ([q[:, sl]],
         [(k_ref[0, :, sl], v_ref[0, :, sl],
           jnp.concatenate([bias_ref[2 * j], bias_ref[2 * j + 1]], axis=0)),
          (kc_ref[:, sl], vc_ref[:, sl], None)],
         None, _pair_store(o_ref, sl))
        for j, sl in enumerate(_pair_slices())])


def _nbr_attn_call(q, k, v, bias_tab, ck, cv):
    b_n, l_n, nq = q.shape
    rows = l_n // GRID_W
    n_rb = rows // NBR_ROWS
    interior = _nbr_pattern(NBR_ROWS, rows)
    assert all(_nbr_pattern(rb * NBR_ROWS, rows) == interior for rb in range(1, n_rb - 1))
    mq = NBR_ROWS * GRID_W
    nk = NBR_WIN * GRID_W
    blk = lambda b, rb: (b, rb, 0)
    ctx_spec = pl.BlockSpec((None,) + ck.shape[1:], lambda b, rb: (b, 0, 0))
    kv_spec = pl.BlockSpec(
        (pl.Element(1), pl.Element(nk), pl.Element(nq)),
        lambda b, rb: (b, _nbr_window_start(rb * NBR_ROWS, rows) * GRID_W, 0))
    bias_spec = pl.BlockSpec((None, N_HEADS, mq, nk),
                             lambda b, rb: (_nbr_class(rb, n_rb), 0, 0, 0))
    return pl.pallas_call(
        _nbr_attn_kernel,
        grid=(b_n, n_rb),
        in_specs=[pl.BlockSpec((None, mq, nq), blk), kv_spec, kv_spec, bias_spec,
                  ctx_spec, ctx_spec],
        out_specs=pl.BlockSpec((None, mq, nq), blk),
        out_shape=jax.ShapeDtypeStruct((b_n, l_n, nq), BF16),
        compiler_params=_cparams(2),
        name="nbr_attn",
    )(q, k, v, bias_tab, ck, cv)


def _block_ffn_kernel(*refs, tiles_per_seq):
    halo = tiles_per_seq > 1
    if halo:
        x_ref, xp_ref, xn_ref, o_ref, op_ref, on_ref = refs[:6]
        refs = refs[6:]
    else:
        x_ref, o_ref = refs[:2]
        refs = refs[2:]
    mod_ref, wo_ref, nw_ref, wup_ref, cw_ref, cb_ref, wdn_ref, y_ref = refs[:8]
    u_refs = refs[8:]
    tm = x_ref.shape[0]
    n_slabs = FF_CHUNK // LANES
    gate_attn = mod_ref[:, 2 * D_MODEL:3 * D_MODEL]
    shift = mod_ref[:, 3 * D_MODEL:4 * D_MODEL]
    scale = mod_ref[:, 4 * D_MODEL:5 * D_MODEL]
    gate_ffn = mod_ref[:, 5 * D_MODEL:6 * D_MODEL]
    if halo:
        i = pl.program_id(1) % tiles_per_seq
        x_ext = jnp.concatenate([x_ref[...], xn_ref[...], xp_ref[...]], axis=0)
        o_nbr = jnp.concatenate([on_ref[...].astype(F32)[:PAD],
                                 op_ref[...].astype(F32)[O_HALO - PAD:]], axis=0)
        o_ext = jnp.concatenate([o_ref[...], o_nbr.astype(BF16)], axis=0)
        x1_ext = x_ext + gate_attn * jnp.dot(o_ext, wo_ref[...], preferred_element_type=F32)
        x1 = x1_ext[:tm]
        h = _rms_modulate(x1_ext, nw_ref[...], shift, scale)
        row = lax.broadcasted_iota(jnp.int32, (2 * PAD, 1), 0)
        inside = ((row < PAD) & (i < tiles_per_seq - 1)) | ((row >= PAD) & (i > 0))
        h = jnp.concatenate([h[:tm], jnp.where(inside, h[tm:], 0.0)], axis=0).astype(BF16)
        placement = ((slice(0, tm), pl.ds(PAD, tm)), (slice(tm, tm + PAD), pl.ds(PAD + tm, PAD)),
                     (slice(tm + PAD, tm + 2 * PAD), pl.ds(0, PAD)))
    else:
        x1 = x_ref[...] + gate_attn * jnp.dot(o_ref[...], wo_ref[...],
                                              preferred_element_type=F32)
        h = _rms_modulate(x1, nw_ref[...], shift, scale).astype(BF16)
        placement = ((slice(0, tm), pl.ds(PAD, tm)),)
        pad = jnp.zeros((PAD, LANES), F32)
        for u_ref in u_refs:
            for s in range(n_slabs):
                u_ref[s, pl.ds(0, PAD), :] = pad
                u_ref[s, pl.ds(PAD + tm, PAD), :] = pad

    def conv(u_ref, col):
        outs = []
        for s in range(n_slabs):
            lanes = slice(col + s * LANES, col + (s + 1) * LANES)
            cw = cw_ref[:, lanes]
            out = cb_ref[:, lanes]
            for o in range(3):
                out = out + u_ref[s, pl.ds(PAD - 1 + o, tm), :] * cw[o:o + 1, :]
            outs.append(out)
        return jnp.concatenate(outs, axis=1)

    def up_one(u_ref, col):
        u = jnp.dot(h, wup_ref[:, col:col + FF_CHUNK], preferred_element_type=F32)
        for s in range(n_slabs):
            for src, dst in placement:
                u_ref[s, dst, :] = u[src, s * LANES:(s + 1) * LANES]

    def up(c):
        up_one(u_refs[2 * (c % FF_SLOTS)], c * FF_CHUNK)
        up_one(u_refs[2 * (c % FF_SLOTS) + 1], D_FF + c * FF_CHUNK)

    n_chunks = D_FF // FF_CHUNK
    acc = jnp.zeros((tm, D_MODEL), F32)
    up(0)
    for c in range(n_chunks):
        if c + 1 < n_chunks:
            up(c + 1)
        ug_ref, uv_ref = u_refs[2 * (c % FF_SLOTS)], u_refs[2 * (c % FF_SLOTS) + 1]
        gc = c * FF_CHUNK
        act = (_silu(conv(ug_ref, gc)) * conv(uv_ref, D_FF + gc)).astype(BF16)
        acc = acc + jnp.dot(act, wdn_ref[gc:gc + FF_CHUNK, :], preferred_element_type=F32)
    y_ref[...] = x1 + gate_ffn * acc


def _block_ffn_call(x, o, mod4, layer, row0, w_o, nw, w_up, conv_w, conv_b, w_down, *,
                    seq_len, tm):
    g_n, l_n, _ = x.shape
    of_layer = lambda g, i: (layer, 0, 0)
    tiles_per_seq = seq_len // tm
    const = lambda g, i: (0, 0)
    tile = lambda g, i: (g, i, 0)

    def with_neighbours(rows):
        specs = [pl.BlockSpec((None, tm, D_MODEL), tile)]
        if tiles_per_seq > 1:
            per_tile, last = tm // rows, l_n // rows - 1
            specs += [
                pl.BlockSpec((None, rows, D_MODEL),
                             lambda g, i: (g, jnp.maximum(i * per_tile - 1, 0), 0)),
                pl.BlockSpec((None, rows, D_MODEL),
                             lambda g, i: (g, jnp.minimum((i + 1) * per_tile, last), 0))]
        return specs

    x_args, o_args = ([x] * 3, [o] * 3) if tiles_per_seq > 1 else ([x], [o])
    return pl.pallas_call(
        functools.partial(_block_ffn_kernel, tiles_per_seq=tiles_per_seq),
        grid=(g_n, l_n // tm),
        in_specs=with_neighbours(PAD) + with_neighbours(O_HALO) + [
            pl.BlockSpec((None, None, 1, 6 * D_MODEL), lambda g, i: (layer, row0 + g, 0, 0)),
            pl.BlockSpec(w_o.shape, const, pipeline_mode=RESIDENT),
            pl.BlockSpec((1, D_MODEL), const),
            pl.BlockSpec((None,) + w_up.shape[1:], of_layer, pipeline_mode=RESIDENT),
            pl.BlockSpec((None,) + conv_w.shape[1:], of_layer),
            pl.BlockSpec((None,) + conv_b.shape[1:], of_layer),
            pl.BlockSpec((None,) + w_down.shape[1:], of_layer, pipeline_mode=RESIDENT),
        ],
        out_specs=pl.BlockSpec((None, tm, D_MODEL), tile),
        out_shape=jax.ShapeDtypeStruct(x.shape, F32),
        scratch_shapes=[pltpu.VMEM((FF_CHUNK // LANES, tm + 2 * PAD, LANES), F32)]
        * (2 * FF_SLOTS),
        compiler_params=_cparams(2),
        name="proj_conv_ffn",
    )(*x_args, *o_args, mod4, w_o, nw, w_up, conv_w, conv_b, w_down)


def _rope_tables(l_n):
    half = HEAD_DIM // 4
    freqs = ROPE_BASE ** (-jnp.arange(half, dtype=F32) / half)
    t = jnp.arange(l_n)
    lane = jnp.arange(HEAD_DIM)
    pos = jnp.where(lane[None, :] < HEAD_DIM // 2, (t // GRID_W)[:, None], (t % GRID_W)[:, None])
    ang = pos.astype(F32) * freqs[lane % half][None, :]
    sign = jnp.where((lane % (2 * half)) < half, -1.0, 1.0).astype(F32)
    cos = jnp.cos(ang)
    sin = jnp.sin(ang) * sign[None, :]
    return jnp.tile(cos, (1, 2)), jnp.tile(sin, (1, 2))


def _nbr_bias_kernel(rpb_ref, o_ref, *, patterns):
    n_dr = 2 * WIN_R - 1
    n_dc = 2 * WIN_C - 1
    base = pl.program_id(0) * (n_dr * n_dc)
    qc = lax.broadcasted_iota(jnp.int32, (GRID_W, LANES), 0)
    lane = lax.broadcasted_iota(jnp.int32, (GRID_W, LANES), 1)
    kc = lane % GRID_W
    cs = jnp.clip(qc - WIN_C // 2, 0, GRID_W - WIN_C)
    col_ok = (kc >= cs) & (kc < cs + WIN_C)
    lo = lane < GRID_W
    seed_rows = 8
    s_row = lax.broadcasted_iota(jnp.int32, (seed_rows, LANES), 0)
    s_lane = lax.broadcasted_iota(jnp.int32, (seed_rows, LANES), 1)
    seed_dc = jnp.clip(s_lane - s_row - GRID_W, -(WIN_C - 1), WIN_C - 1) + WIN_C - 1
    seed_hit = [seed_dc == d for d in range(n_dc)]
    masked = jnp.full((GRID_W, LANES), NEG, F32)
    blocks = []
    for a in range(n_dr):
        seed = jnp.zeros((seed_rows, LANES), F32)
        for d in range(n_dc):
            seed = jnp.where(seed_hit[d], rpb_ref[base + a * n_dc + d], seed)
        t = jnp.concatenate([pltpu.roll(seed, GRID_W + seed_rows * r, axis=1)
                             for r in range(GRID_W // seed_rows)], axis=0)
        t = jnp.where(col_ok, t, NEG)
        blocks.append(jnp.where(lo, t, pltpu.roll(t, GRID_W, axis=1)))
    pick = lambda a: masked if a is None else blocks[a]
    for cls, pat in enumerate(patterns):
        for i in range(NBR_ROWS):
            for m in range(NBR_WIN // 2):
                o_ref[cls, i * GRID_W:(i + 1) * GRID_W, m * LANES:(m + 1) * LANES] = jnp.where(
                    lo, pick(pat[i][2 * m]), pick(pat[i][2 * m + 1]))


def _nbr_bias_table(rpb, rows):
    n_heads = rpb.shape[0]
    patterns = tuple(_nbr_pattern(r0, rows) for r0 in (0, NBR_ROWS, rows - NBR_ROWS))
    mq = NBR_ROWS * GRID_W
    nk = NBR_WIN * GRID_W
    return pl.pallas_call(
        functools.partial(_nbr_bias_kernel, patterns=patterns),
        grid=(n_heads,),
        in_specs=[pl.BlockSpec(memory_space=pltpu.SMEM)],
        out_specs=pl.BlockSpec((len(patterns), None, mq, nk), lambda h: (0, h, 0, 0)),
        out_shape=jax.ShapeDtypeStruct((len(patterns), n_heads, mq, nk), F32),
        compiler_params=_cparams(1),
        name="nbr_bias",
    )(rpb.astype(F32).reshape(-1))


def _block_diag_mean():
    r = jnp.arange(NORM_CHUNK) // HEAD_DIM
    return jnp.where(r[:, None] == r[None, :], 1.0 / HEAD_DIM, 0.0).astype(BF16)


def kernel(x_prompt, x_sample, cache_k_a, cache_v_a, cache_k_b, cache_v_b, c, c_ctx,
           norm_attn_w, norm_ffn_w, w_ada, b_ada,
           w_qkv_a, q_norm_a, k_norm_a, sink_a, w_o_a,
           w_qkv_b, q_norm_b, k_norm_b, rpb_b, w_o_b,
           w_up, conv_w, conv_b, w_down):
    depth = w_ada.shape[0]
    batch, seq, _ = x_prompt.shape
    dec_batch, dec_seq, _ = x_sample.shape
    nq = N_HEADS * HEAD_DIM

    cond = jnp.concatenate(
        [c_ctx[None, :], c, jnp.zeros((N_MOD_ROWS - 1 - dec_batch, D_MODEL), F32)], axis=0)
    mod = _ada_call(cond.T, 1 + dec_batch, w_ada, b_ada[:, None, :])
    mod4 = mod.reshape(depth, N_MOD_ROWS, 1, 6 * D_MODEL)

    bd = _block_diag_mean()
    rope_tabs = _rope_tables(dec_seq)
    w_up_bf = w_up.astype(BF16)
    w_down_bf = w_down.astype(BF16)
    tile_w = lambda w, n: jnp.tile(w, n)[None, :]

    xp = x_prompt.reshape(1, batch * seq, D_MODEL)
    xs = x_sample
    new_k_a, new_v_a, new_k_b, new_v_b = [], [], [], []
    for i in range(depth):
        j = i // 2
        nw_attn = norm_attn_w[i][None, :]
        nw_ffn = norm_ffn_w[i][None, :]
        if i % 2 == 0:
            nkv = N_KV_A * HEAD_DIM
            w_qkv = w_qkv_a[j].astype(BF16)
            w_o = w_o_a[j].astype(BF16)
            qn, kn = tile_w(q_norm_a[j], N_HEADS), tile_w(k_norm_a[j], N_KV_A)
            q, k, v, kd, vd = _qkv_call(xp, mod4, i, 0, nw_attn, w_qkv, qn, kn, bd, None,
                                        nkv=nkv, dup=True, f32_kv=True, tm=512)
            new_k_a.append(k.reshape(batch, seq, N_KV_A, HEAD_DIM))
            new_v_a.append(v.reshape(batch, seq, N_KV_A, HEAD_DIM))
            shp = (batch, seq, -1)
            op = _ctx_attn_call(q.reshape(shp), kd.reshape(shp), vd.reshape(shp), sink_a[j])
            q, kd, vd = _qkv_call(xs, mod4, i, 1, nw_attn, w_qkv, qn, kn, bd, rope_tabs,
                                  nkv=nkv, dup=True, f32_kv=False, tm=512)
            o_s = _win_attn_call(q, kd, vd,
                                 cache_k_a[:, j].reshape(dec_batch, -1, nkv),
                                 cache_v_a[:, j].reshape(dec_batch, -1, nkv), sink_a[j])
        else:
            nkv = nq
            w_qkv = w_qkv_b[j].astype(BF16)
            w_o = w_o_b[j].astype(BF16)
            qn, kn = tile_w(q_norm_b[j], N_HEADS), tile_w(k_norm_b[j], N_HEADS)
            q, k, v = _qkv_call(xp, mod4, i, 0, nw_attn, w_qkv, qn, kn, bd, None,
                                nkv=nkv, dup=False, f32_kv=True, tm=512)
            new_k_b.append(k.reshape(batch, seq, N_HEADS, HEAD_DIM))
            new_v_b.append(v.reshape(batch, seq, N_HEADS, HEAD_DIM))
            shp = (batch, seq, -1)
            op = _ctx_attn_call(q.reshape(shp), k.reshape(shp), v.reshape(shp), None)
            q, k, v = _qkv_call(xs, mod4, i, 1, nw_attn, w_qkv, qn, kn, bd, None,
                                nkv=nkv, dup=False, f32_kv=False, tm=512)
            o_s = _nbr_attn_call(q, k, v, _nbr_bias_table(rpb_b[j], dec_seq // GRID_W),
                                 cache_k_b[:, j].reshape(dec_batch, -1, nq).astype(BF16),
                                 cache_v_b[:, j].reshape(dec_batch, -1, nq).astype(BF16))
        blk_w = (w_o, nw_ffn, w_up_bf, conv_w, conv_b[:, None, :], w_down_bf)
        xp = _block_ffn_call(xp, op.reshape(xp.shape), mod4, i, 0, *blk_w, seq_len=seq, tm=seq)
        xs = _block_ffn_call(xs, o_s, mod4, i, 1, *blk_w, seq_len=dec_seq, tm=512)
    return (xp.reshape(batch, seq, D_MODEL), xs,
            jnp.stack(new_k_a, axis=1), jnp.stack(new_v_a, axis=1),
            jnp.stack(new_k_b, axis=1), jnp.stack(new_v_b, axis=1))
```

```python
import functools

import jax
import jax.numpy as jnp
from jax import lax
from jax.experimental import pallas as pl
from jax.experimental.pallas import tpu as pltpu

D_MODEL = 1024
HEAD_DIM = 64
N_HEADS = 16
N_KV_A = 4
GRID_W = 64
WINDOW = 128
WIN_R = 8
WIN_C = 16
D_FF = 2816
ROPE_BASE = 10000.0
EPS = 1e-6
SCALE = HEAD_DIM ** -0.5

LANES = 128
PAD = 8
O_HALO = 16
NBR_ROWS = 4
NBR_WIN = NBR_ROWS + WIN_R
NORM_CHUNK = 256
FF_CHUNK = 256
FF_SLOTS = 3
NEG = -1e30
N_MOD_ROWS = 8

RESIDENT = pl.Buffered(1)

F32 = jnp.float32
BF16 = jnp.bfloat16
VMEM_LIMIT = 56 * 1024 * 1024


def _cparams(n_axes):
    return pltpu.CompilerParams(
        dimension_semantics=("arbitrary",) * n_axes, vmem_limit_bytes=VMEM_LIMIT)


def _silu(x):
    return x * (1.0 / (1.0 + jnp.exp(-x)))


def _rms_modulate(x, nw, shift, scale):
    ms = jnp.mean(x * x, axis=-1, keepdims=True)
    return x * lax.rsqrt(ms + EPS) * (nw * (1.0 + scale)) + shift


def _ada_kernel(ct_ref, w_ref, b_ref, o_ref, *, n_cond):
    a = _silu(ct_ref[...])
    w = w_ref[...]
    rows = [jnp.sum(a[:, r:r + 1] * w, axis=0, keepdims=True) for r in range(n_cond)]
    rows.append(jnp.zeros((N_MOD_ROWS - n_cond, w.shape[1]), F32))
    o_ref[...] = jnp.concatenate(rows, axis=0) + b_ref[...]


def _ada_call(cond_t, n_cond, w_ada, b_ada):
    depth = w_ada.shape[0]
    tn = 2048
    return pl.pallas_call(
        functools.partial(_ada_kernel, n_cond=n_cond),
        grid=(depth, 6 * D_MODEL // tn),
        in_specs=[
            pl.BlockSpec((D_MODEL, N_MOD_ROWS), lambda l, n: (0, 0)),
            pl.BlockSpec((None, D_MODEL, tn), lambda l, n: (l, 0, n)),
            pl.BlockSpec((None, 1, tn), lambda l, n: (l, 0, n)),
        ],
        out_specs=pl.BlockSpec((None, N_MOD_ROWS, tn), lambda l, n: (l, 0, n)),
        out_shape=jax.ShapeDtypeStruct((depth, N_MOD_ROWS, 6 * D_MODEL), F32),
        compiler_params=_cparams(2),
        name="adaln",
    )(cond_t, w_ada, b_ada)


def _head_rms(t, bd, w):
    outs = []
    for c in range(t.shape[1] // NORM_CHUNK):
        tc = t[:, c * NORM_CHUNK:(c + 1) * NORM_CHUNK]
        ms = jnp.dot((tc * tc).astype(BF16), bd, preferred_element_type=F32)
        outs.append(tc * lax.rsqrt(ms + EPS))
    return jnp.concatenate(outs, axis=1) * w


def _rope(t, cos, sin_signed, lo16):
    outs = []
    for c in range(t.shape[1] // LANES):
        tc = t[:, c * LANES:(c + 1) * LANES]
        partner = jnp.where(lo16, pltpu.roll(tc, LANES - 16, axis=1), pltpu.roll(tc, 16, axis=1))
        outs.append(tc * cos + partner * sin_signed)
    return jnp.concatenate(outs, axis=1)


def _dup_heads(t, lo64):
    outs = []
    for c in range(t.shape[1] // LANES):
        tc = t[:, c * LANES:(c + 1) * LANES]
        sw = pltpu.roll(tc, HEAD_DIM, axis=1)
        outs.append(jnp.where(lo64, tc, sw))
        outs.append(jnp.where(lo64, sw, tc))
    return jnp.concatenate(outs, axis=1)


def _qkv_kernel(*refs, nkv, rope, dup, f32_kv):
    x_ref, mod_ref, nw_ref, w_ref, qn_ref, kn_ref, bd_ref = refs[:7]
    pos = 7
    if rope:
        cos_ref, sin_ref = refs[pos:pos + 2]
        pos += 2
    outs = refs[pos:]
    nq = N_HEADS * HEAD_DIM
    lane = lax.broadcasted_iota(jnp.int32, (1, LANES), 1)

    h = _rms_modulate(x_ref[...], nw_ref[...], mod_ref[:, 0:D_MODEL],
                      mod_ref[:, D_MODEL:2 * D_MODEL])
    qkv = jnp.dot(h.astype(BF16), w_ref[...], preferred_element_type=F32)
    bd = bd_ref[...]
    q = _head_rms(qkv[:, :nq], bd, qn_ref[...])
    k = _head_rms(qkv[:, nq:nq + nkv], bd, kn_ref[...])
    v = qkv[:, nq + nkv:]
    if rope:
        lo16 = (lane & 16) == 0
        q = _rope(q, cos_ref[...], sin_ref[...], lo16)
        k = _rope(k, cos_ref[...], sin_ref[...], lo16)
    outs[0][...] = (q * SCALE).astype(BF16)
    o = 1
    if f32_kv:
        outs[o][...] = k
        outs[o + 1][...] = v
        o += 2
    if dup:
        lo64 = lane < HEAD_DIM
        k = _dup_heads(k, lo64)
        v = _dup_heads(v, lo64)
    if dup or not f32_kv:
        outs[o][...] = k.astype(BF16)
        outs[o + 1][...] = v.astype(BF16)


def _qkv_call(x, mod4, layer, row0, nw, w, qn, kn, bd, rope_tabs, *, nkv, dup, f32_kv, tm):
    g_n, l_n, _ = x.shape
    rope = rope_tabs is not None
    nq = N_HEADS * HEAD_DIM
    xmap = lambda g, i: (g, i, 0)
    const = lambda g, i: (0, 0)
    in_specs = [
        pl.BlockSpec((None, tm, D_MODEL), xmap),
        pl.BlockSpec((None, None, 1, 6 * D_MODEL), lambda g, i: (layer, row0 + g, 0, 0)),
        pl.BlockSpec((1, D_MODEL), const),
        pl.BlockSpec(w.shape, const),
        pl.BlockSpec((1, nq), const),
        pl.BlockSpec((1, nkv), const),
        pl.BlockSpec((NORM_CHUNK, NORM_CHUNK), const),
    ]
    args = [x, mod4, nw, w, qn, kn, bd]
    if rope:
        in_specs += [pl.BlockSpec((tm, LANES), lambda g, i: (i, 0))] * 2
        args += list(rope_tabs)
    out_shape = [jax.ShapeDtypeStruct((g_n, l_n, nq), BF16)]
    out_specs = [pl.BlockSpec((None, tm, nq), xmap)]
    if f32_kv:
        out_shape += [jax.ShapeDtypeStruct((g_n, l_n, nkv), F32)] * 2
        out_specs += [pl.BlockSpec((None, tm, nkv), xmap)] * 2
    if dup or not f32_kv:
        wkv = nkv * 2 if dup else nkv
        out_shape += [jax.ShapeDtypeStruct((g_n, l_n, wkv), BF16)] * 2
        out_specs += [pl.BlockSpec((None, tm, wkv), xmap)] * 2
    return pl.pallas_call(
        functools.partial(_qkv_kernel, nkv=nkv, rope=rope, dup=dup, f32_kv=f32_kv),
        grid=(g_n, l_n // tm),
        in_specs=in_specs,
        out_specs=out_specs,
        out_shape=out_shape,
        compiler_params=_cparams(2),
        name="qkv",
    )(*args)


def _scores(q2_list, segs):
    lo = lax.broadcasted_iota(jnp.int32, (1, LANES), 1) < HEAD_DIM
    zero = jnp.zeros((), BF16)
    stacked = []
    for q2 in q2_list:
        stacked.append(jnp.where(lo, q2, zero))
        stacked.append(jnp.where(lo, zero, q2))
    qs = jnp.concatenate(stacked, axis=0)
    scores = []
    for k2, _, bias in segs:
        s = lax.dot_general(qs, k2, (((1,), (1,)), ((), ())), preferred_element_type=F32)
        scores.append(s if bias is None else s + bias)
    return scores


def _softmax_pv(scores, segs, sinks, m_rows):
    lo = lax.broadcasted_iota(jnp.int32, (1, LANES), 1) < HEAD_DIM
    lane_tiles = lambda t: [t[:, c * LANES:(c + 1) * LANES] for c in range(t.shape[1] // LANES)]
    mx_tile = functools.reduce(jnp.maximum, [t for s in scores for t in lane_tiles(s)])
    if sinks is not None:
        sink_tile = jnp.concatenate(
            [jnp.full((m_rows, LANES), sk, F32) for sk in sinks], axis=0)
        mx_tile = jnp.maximum(mx_tile, sink_tile)
    mx = jnp.max(mx_tile, axis=-1, keepdims=True)
    den_tile = (jnp.exp(sink_tile - mx) * (1.0 / LANES) if sinks is not None
                else jnp.zeros((scores[0].shape[0], LANES), F32))
    acc = jnp.zeros((scores[0].shape[0], LANES), F32)
    for s, (_, v2, _) in zip(scores, segs):
        p = jnp.exp(s - mx)
        den_tile = functools.reduce(jnp.add, lane_tiles(p), den_tile)
        acc = acc + jnp.dot(p.astype(BF16), v2, preferred_element_type=F32)
    rows = lambda t, n: t[n * m_rows:(n + 1) * m_rows]
    out = acc / jnp.sum(den_tile, axis=-1, keepdims=True)
    return [jnp.where(lo, rows(out, 2 * i), rows(out, 2 * i + 1))
            for i in range(out.shape[0] // (2 * m_rows))]


def _run_attention(jobs, lookahead=True):
    scores = _scores(*jobs[0][:2])
    for n, (q2_list, segs, sinks, store) in enumerate(jobs):
        more = n + 1 < len(jobs)
        if lookahead and more:
            nxt = _scores(*jobs[n + 1][:2])
        store(_softmax_pv(scores, segs, sinks, q2_list[0].shape[0]))
        if more:
            scores = nxt if lookahead else _scores(*jobs[n + 1][:2])


def _pair_slices():
    return [slice(j * LANES, (j + 1) * LANES) for j in range(N_HEADS // 2)]


def _pair_store(o_ref, sl):
    def _store(outs):
        o_ref[:, sl] = outs[0].astype(o_ref.dtype)
    return _store


def _dup_f32_to_bf16(t):
    lo64 = lax.broadcasted_iota(jnp.int32, (1, LANES), 1) < HEAD_DIM
    return _dup_heads(t, lo64).astype(BF16)


def _gqa_heads(q, segs_of_group, sink_ref, o_ref, lookahead):
    pairs_per_job = 2
    def store(pairs):
        def _store(outs):
            for p, out in zip(pairs, outs):
                o_ref[:, p * LANES:(p + 1) * LANES] = out.astype(o_ref.dtype)
        return _store

    jobs = []
    for g in range(N_KV_A):
        for first in range(2 * g, 2 * g + 2, pairs_per_job):
            pairs = list(range(first, first + pairs_per_job))
            jobs.append(([q[:, p * LANES:(p + 1) * LANES] for p in pairs], segs_of_group(g),
                         [sink_ref[2 * p + i] for p in pairs for i in range(2)], store(pairs)))
    _run_attention(jobs, lookahead)


def _ctx_attn_a_kernel(sink_ref, q_ref, k_ref, v_ref, o_ref):
    tile = lambda t, g: t[:, g * LANES:(g + 1) * LANES]
    for e in range(q_ref.shape[0]):
        k = k_ref[e]
        v = v_ref[e]
        _gqa_heads(q_ref[e], lambda g: [(tile(k, g), tile(v, g), None)], sink_ref,
                   o_ref.at[e], True)


def _ctx_attn_b_kernel(q_ref, k_ref, v_ref, o_ref):
    jobs = []
    for e in range(q_ref.shape[0]):
        q = q_ref[e]
        k = k_ref[e].astype(BF16)
        v = v_ref[e].astype(BF16)
        jobs += [([q[:, sl]], [(k[:, sl], v[:, sl], None)], None, _pair_store(o_ref.at[e], sl))
                 for sl in _pair_slices()]
    _run_attention(jobs)


def _ctx_attn_call(q, k, v, sink):
    b_n, l_n, nq = q.shape
    bmap = lambda b: (b, 0, 0)
    per_step = 2
    assert b_n % per_step == 0
    in_specs = [pl.BlockSpec((per_step, l_n, nq), bmap),
                pl.BlockSpec((per_step, l_n, k.shape[2]), bmap),
                pl.BlockSpec((per_step, l_n, v.shape[2]), bmap)]
    args = [q, k, v]
    kern = _ctx_attn_b_kernel
    if sink is not None:
        in_specs = [pl.BlockSpec(memory_space=pltpu.SMEM)] + in_specs
        args = [sink] + args
        kern = _ctx_attn_a_kernel
    return pl.pallas_call(
        kern,
        grid=(b_n // per_step,),
        in_specs=in_specs,
        out_specs=pl.BlockSpec((per_step, l_n, nq), bmap),
        out_shape=jax.ShapeDtypeStruct((b_n, l_n, nq), BF16),
        compiler_params=_cparams(1),
        name="ctx_attn",
    )(*args)


def _win_attn_kernel(sink_ref, q_ref, kp_ref, kc_ref, kn_ref, vp_ref, vc_ref, vn_ref,
                     ck_ref, cv_ref, o_ref, kctx_ref, vctx_ref, bias_ref):
    i = pl.program_id(1)
    nb = pl.num_programs(1)
    blk = q_ref.shape[0]

    @pl.when(i == 0)
    def _():
        kctx_ref[...] = _dup_f32_to_bf16(ck_ref[...])
        vctx_ref[...] = _dup_f32_to_bf16(cv_ref[...])
        a = lax.broadcasted_iota(jnp.int32, bias_ref.shape[1:], 0) % blk
        j = lax.broadcasted_iota(jnp.int32, bias_ref.shape[1:], 1)
        band = (j >= a) & (j <= a + 2 * WINDOW)
        for cls, ok in enumerate((band & (j >= blk), band, band & (j < 2 * blk))):
            bias_ref[cls] = jnp.where(ok, 0.0, NEG).astype(F32)

    q = q_ref[...]
    k_loc = jnp.concatenate([kp_ref[...], kc_ref[...], kn_ref[...]], axis=0)
    v_loc = jnp.concatenate([vp_ref[...], vc_ref[...], vn_ref[...]], axis=0)
    bias = bias_ref[jnp.where(i == 0, 0, jnp.where(i == nb - 1, 2, 1))]
    tile = lambda t, g: t[:, g * LANES:(g + 1) * LANES]
    _gqa_heads(q, lambda g: [(tile(k_loc, g), tile(v_loc, g), bias),
                             (kctx_ref[:, g * LANES:(g + 1) * LANES],
                              vctx_ref[:, g * LANES:(g + 1) * LANES], None)],
               sink_ref, o_ref, True)


def _win_attn_call(q, kd, vd, ck, cv, sink):
    b_n, l_n, nq = q.shape
    blk = WINDOW
    nb = l_n // blk
    assert nb >= 3
    wkv = kd.shape[2]
    prev = lambda b, i: (b, jnp.maximum(i - 1, 0), 0)
    cur = lambda b, i: (b, i, 0)
    nxt = lambda b, i: (b, jnp.minimum(i + 1, nb - 1), 0)
    kv_specs = [pl.BlockSpec((None, blk, wkv), m) for m in (prev, cur, nxt)]
    ctx_spec = pl.BlockSpec((None,) + ck.shape[1:], lambda b, i: (b, 0, 0))
    return pl.pallas_call(
        _win_attn_kernel,
        grid=(b_n, nb),
        in_specs=[pl.BlockSpec(memory_space=pltpu.SMEM),
                  pl.BlockSpec((None, blk, nq), cur)] + kv_specs + kv_specs + [ctx_spec, ctx_spec],
        out_specs=pl.BlockSpec((None, blk, nq), cur),
        out_shape=jax.ShapeDtypeStruct((b_n, l_n, nq), BF16),
        scratch_shapes=[pltpu.VMEM((ck.shape[1], wkv), BF16)] * 2
        + [pltpu.VMEM((3, N_HEADS // N_KV_A * blk, 3 * blk), F32)],
        compiler_params=_cparams(2),
        name="win_attn",
    )(sink, q, kd, kd, kd, vd, vd, vd, ck, cv)


def _nbr_window_start(r0, rows):
    return jnp.clip(r0 - WIN_R // 2, 0, rows - NBR_WIN)


def _nbr_pattern(r0, rows):
    ws = min(max(r0 - WIN_R // 2, 0), rows - NBR_WIN)
    pat = []
    for i in range(NBR_ROWS):
        r = r0 + i
        rs = min(max(r - WIN_R // 2, 0), rows - WIN_R)
        pat.append(tuple((ws + w - r + WIN_R - 1) if rs <= ws + w < rs + WIN_R else None
                         for w in range(NBR_WIN)))
    return tuple(pat)


def _nbr_class(rb, n_rb):
    return jnp.where(rb == 0, 0, jnp.where(rb == n_rb - 1, 2, 1))


def _nbr_attn_kernel(q_ref, k_ref, v_ref, bias_ref, kc_ref, vc_ref, o_ref):
    q = q_ref[...]
    _run_attention([
        ([q[:, sl]],
         [(k_ref[0, :, sl], v_ref[0, :, sl],
           jnp.concatenate([bias_ref[2 * j], bias_ref[2 * j + 1]], axis=0)),
          (kc_ref[:, sl], vc_ref[:, sl], None)],
         None, _pair_store(o_ref, sl))
        for j, sl in enumerate(_pair_slices())])


def _nbr_attn_call(q, k, v, bias_tab, ck, cv):
    b_n, l_n, nq = q.shape
    rows = l_n // GRID_W
    n_rb = rows // NBR_ROWS
    interior = _nbr_pattern(NBR_ROWS, rows)
    assert all(_nbr_pattern(rb * NBR_ROWS, rows) == interior for rb in range(1, n_rb - 1))
    mq = NBR_ROWS * GRID_W
    nk = NBR_WIN * GRID_W
    blk = lambda b, rb: (b, rb, 0)
    ctx_spec = pl.BlockSpec((None,) + ck.shape[1:], lambda b, rb: (b, 0, 0))
    kv_spec = pl.BlockSpec(
        (pl.Element(1), pl.Element(nk), pl.Element(nq)),
        lambda b, rb: (b, _nbr_window_start(rb * NBR_ROWS, rows) * GRID_W, 0))
    bias_spec = pl.BlockSpec((None, N_HEADS, mq, nk),
                             lambda b, rb: (_nbr_class(rb, n_rb), 0, 0, 0))
    return pl.pallas_call(
        _nbr_attn_kernel,
        grid=(b_n, n_rb),
        in_specs=[pl.BlockSpec((None, mq, nq), blk), kv_spec, kv_spec, bias_spec,
                  ctx_spec, ctx_spec],
        out_specs=pl.BlockSpec((None, mq, nq), blk),
        out_shape=jax.ShapeDtypeStruct((b_n, l_n, nq), BF16),
        compiler_params=_cparams(2),
        name="nbr_attn",
    )(q, k, v, bias_tab, ck, cv)


def _block_ffn_kernel(*refs, tiles_per_seq):
    halo = tiles_per_seq > 1
    if halo:
        x_ref, xp_ref, xn_ref, o_ref, op_ref, on_ref = refs[:6]
        refs = refs[6:]
    else:
        x_ref, o_ref = refs[:2]
        refs = refs[2:]
    mod_ref, wo_ref, nw_ref, wup_ref, cw_ref, cb_ref, wdn_ref, y_ref = refs[:8]
    u_refs = refs[8:]
    tm = x_ref.shape[0]
    n_slabs = FF_CHUNK // LANES
    gate_attn = mod_ref[:, 2 * D_MODEL:3 * D_MODEL]
    shift = mod_ref[:, 3 * D_MODEL:4 * D_MODEL]
    scale = mod_ref[:, 4 * D_MODEL:5 * D_MODEL]
    gate_ffn = mod_ref[:, 5 * D_MODEL:6 * D_MODEL]
    if halo:
        i = pl.program_id(1) % tiles_per_seq
        x_ext = jnp.concatenate([x_ref[...], xn_ref[...], xp_ref[...]], axis=0)
        o_nbr = jnp.concatenate([on_ref[...].astype(F32)[:PAD],
                                 op_ref[...].astype(F32)[O_HALO - PAD:]], axis=0)
        o_ext = jnp.concatenate([o_ref[...], o_nbr.astype(BF16)], axis=0)
        x1_ext = x_ext + gate_attn * jnp.dot(o_ext, wo_ref[...], preferred_element_type=F32)
        x1 = x1_ext[:tm]
        h = _rms_modulate(x1_ext, nw_ref[...], shift, scale)
        row = lax.broadcasted_iota(jnp.int32, (2 * PAD, 1), 0)
        inside = ((row < PAD) & (i < tiles_per_seq - 1)) | ((row >= PAD) & (i > 0))
        h = jnp.concatenate([h[:tm], jnp.where(inside, h[tm:], 0.0)], axis=0).astype(BF16)
        placement = ((slice(0, tm), pl.ds(PAD, tm)), (slice(tm, tm + PAD), pl.ds(PAD + tm, PAD)),
                     (slice(tm + PAD, tm + 2 * PAD), pl.ds(0, PAD)))
    else:
        x1 = x_ref[...] + gate_attn * jnp.dot(o_ref[...], wo_ref[...],
                                              preferred_element_type=F32)
        h = _rms_modulate(x1, nw_ref[...], shift, scale).astype(BF16)
        placement = ((slice(0, tm), pl.ds(PAD, tm)),)
        pad = jnp.zeros((PAD, LANES), F32)
        for u_ref in u_refs:
            for s in range(n_slabs):
                u_ref[s, pl.ds(0, PAD), :] = pad
                u_ref[s, pl.ds(PAD + tm, PAD), :] = pad

    def conv(u_ref, col):
        outs = []
        for s in range(n_slabs):
            lanes = slice(col + s * LANES, col + (s + 1) * LANES)
            cw = cw_ref[:, lanes]
            out = cb_ref[:, lanes]
            for o in range(3):
                out = out + u_ref[s, pl.ds(PAD - 1 + o, tm), :] * cw[o:o + 1, :]
            outs.append(out)
        return jnp.concatenate(outs, axis=1)

    def up_one(u_ref, col):
        u = jnp.dot(h, wup_ref[:, col:col + FF_CHUNK], preferred_element_type=F32)
        for s in range(n_slabs):
            for src, dst in placement:
                u_ref[s, dst, :] = u[src, s * LANES:(s + 1) * LANES]

    def up(c):
        up_one(u_refs[2 * (c % FF_SLOTS)], c * FF_CHUNK)
        up_one(u_refs[2 * (c % FF_SLOTS) + 1], D_FF + c * FF_CHUNK)

    n_chunks = D_FF // FF_CHUNK
    acc = jnp.zeros((tm, D_MODEL), F32)
    up(0)
    for c in range(n_chunks):
        if c + 1 < n_chunks:
            up(c + 1)
        ug_ref, uv_ref = u_refs[2 * (c % FF_SLOTS)], u_refs[2 * (c % FF_SLOTS) + 1]
        gc = c * FF_CHUNK
        act = (_silu(conv(ug_ref, gc)) * conv(uv_ref, D_FF + gc)).astype(BF16)
        acc = acc + jnp.dot(act, wdn_ref[gc:gc + FF_CHUNK, :], preferred_element_type=F32)
    y_ref[...] = x1 + gate_ffn * acc


def _block_ffn_call(x, o, mod4, layer, row0, w_o, nw, w_up, conv_w, conv_b, w_down, *,
                    seq_len, tm):
    g_n, l_n, _ = x.shape
    of_layer = lambda g, i: (layer, 0, 0)
    tiles_per_seq = seq_len // tm
    const = lambda g, i: (0, 0)
    tile = lambda g, i: (g, i, 0)

    def with_neighbours(rows):
        specs = [pl.BlockSpec((None, tm, D_MODEL), tile)]
        if tiles_per_seq > 1:
            per_tile, last = tm // rows, l_n // rows - 1
            specs += [
                pl.BlockSpec((None, rows, D_MODEL),
                             lambda g, i: (g, jnp.maximum(i * per_tile - 1, 0), 0)),
                pl.BlockSpec((None, rows, D_MODEL),
                             lambda g, i: (g, jnp.minimum((i + 1) * per_tile, last), 0))]
        return specs

    x_args, o_args = ([x] * 3, [o] * 3) if tiles_per_seq > 1 else ([x], [o])
    return pl.pallas_call(
        functools.partial(_block_ffn_kernel, tiles_per_seq=tiles_per_seq),
        grid=(g_n, l_n // tm),
        in_specs=with_neighbours(PAD) + with_neighbours(O_HALO) + [
            pl.BlockSpec((None, None, 1, 6 * D_MODEL), lambda g, i: (layer, row0 + g, 0, 0)),
            pl.BlockSpec(w_o.shape, const, pipeline_mode=RESIDENT),
            pl.BlockSpec((1, D_MODEL), const),
            pl.BlockSpec((None,) + w_up.shape[1:], of_layer, pipeline_mode=RESIDENT),
            pl.BlockSpec((None,) + conv_w.shape[1:], of_layer),
            pl.BlockSpec((None,) + conv_b.shape[1:], of_layer),
            pl.BlockSpec((None,) + w_down.shape[1:], of_layer, pipeline_mode=RESIDENT),
        ],
        out_specs=pl.BlockSpec((None, tm, D_MODEL), tile),
        out_shape=jax.ShapeDtypeStruct(x.shape, F32),
        scratch_shapes=[pltpu.VMEM((FF_CHUNK // LANES, tm + 2 * PAD, LANES), F32)]
        * (2 * FF_SLOTS),
        compiler_params=_cparams(2),
        name="proj_conv_ffn",
    )(*x_args, *o_args, mod4, w_o, nw, w_up, conv_w, conv_b, w_down)


def _rope_tables(l_n):
    half = HEAD_DIM // 4
    freqs = ROPE_BASE ** (-jnp.arange(half, dtype=F32) / half)
    t = jnp.arange(l_n)
    lane = jnp.arange(HEAD_DIM)
    pos = jnp.where(lane[None, :] < HEAD_DIM // 2, (t // GRID_W)[:, None], (t % GRID_W)[:, None])
    ang = pos.astype(F32) * freqs[lane % half][None, :]
    sign = jnp.where((lane % (2 * half)) < half, -1.0, 1.0).astype(F32)
    cos = jnp.cos(ang)
    sin = jnp.sin(ang) * sign[None, :]
    return jnp.tile(cos, (1, 2)), jnp.tile(sin, (1, 2))


def _nbr_bias_kernel(rpb_ref, o_ref, *, patterns):
    n_dr = 2 * WIN_R - 1
    n_dc = 2 * WIN_C - 1
    base = pl.program_id(0) * (n_dr * n_dc)
    qc = lax.broadcasted_iota(jnp.int32, (GRID_W, LANES), 0)
    lane = lax.broadcasted_iota(jnp.int32, (GRID_W, LANES), 1)
    kc = lane % GRID_W
    cs = jnp.clip(qc - WIN_C // 2, 0, GRID_W - WIN_C)
    col_ok = (kc >= cs) & (kc < cs + WIN_C)
    lo = lane < GRID_W
    seed_rows = 8
    s_row = lax.broadcasted_iota(jnp.int32, (seed_rows, LANES), 0)
    s_lane = lax.broadcasted_iota(jnp.int32, (seed_rows, LANES), 1)
    seed_dc = jnp.clip(s_lane - s_row - GRID_W, -(WIN_C - 1), WIN_C - 1) + WIN_C - 1
    seed_hit = [seed_dc == d for d in range(n_dc)]
    masked = jnp.full((GRID_W, LANES), NEG, F32)
    blocks = []
    for a in range(n_dr):
        seed = jnp.zeros((seed_rows, LANES), F32)
        for d in range(n_dc):
            seed = jnp.where(seed_hit[d], rpb_ref[base + a * n_dc + d], seed)
        t = jnp.concatenate([pltpu.roll(seed, GRID_W + seed_rows * r, axis=1)
                             for r in range(GRID_W // seed_rows)], axis=0)
        t = jnp.where(col_ok, t, NEG)
        blocks.append(jnp.where(lo, t, pltpu.roll(t, GRID_W, axis=1)))
    pick = lambda a: masked if a is None else blocks[a]
    for cls, pat in enumerate(patterns):
        for i in range(NBR_ROWS):
            for m in range(NBR_WIN // 2):
                o_ref[cls, i * GRID_W:(i + 1) * GRID_W, m * LANES:(m + 1) * LANES] = jnp.where(
                    lo, pick(pat[i][2 * m]), pick(pat[i][2 * m + 1]))


def _nbr_bias_table(rpb, rows):
    n_heads = rpb.shape[0]
    patterns = tuple(_nbr_pattern(r0, rows) for r0 in (0, NBR_ROWS, rows - NBR_ROWS))
    mq = NBR_ROWS * GRID_W
    nk = NBR_WIN * GRID_W
    return pl.pallas_call(
        functools.partial(_nbr_bias_kernel, patterns=patterns),
        grid=(n_heads,),
        in_specs=[pl.BlockSpec(memory_space=pltpu.SMEM)],
        out_specs=pl.BlockSpec((len(patterns), None, mq, nk), lambda h: (0, h, 0, 0)),
        out_shape=jax.ShapeDtypeStruct((len(patterns), n_heads, mq, nk), F32),
        compiler_params=_cparams(1),
        name="nbr_bias",
    )(rpb.astype(F32).reshape(-1))


def _block_diag_mean():
    r = jnp.arange(NORM_CHUNK) // HEAD_DIM
    return jnp.where(r[:, None] == r[None, :], 1.0 / HEAD_DIM, 0.0).astype(BF16)


def kernel(x_prompt, x_sample, cache_k_a, cache_v_a, cache_k_b, cache_v_b, c, c_ctx,
           norm_attn_w, norm_ffn_w, w_ada, b_ada,
           w_qkv_a, q_norm_a, k_norm_a, sink_a, w_o_a,
           w_qkv_b, q_norm_b, k_norm_b, rpb_b, w_o_b,
           w_up, conv_w, conv_b, w_down):
    depth = w_ada.shape[0]
    batch, seq, _ = x_prompt.shape
    dec_batch, dec_seq, _ = x_sample.shape
    nq = N_HEADS * HEAD_DIM

    cond = jnp.concatenate(
        [c_ctx[None, :], c, jnp.zeros((N_MOD_ROWS - 1 - dec_batch, D_MODEL), F32)], axis=0)
    mod = _ada_call(cond.T, 1 + dec_batch, w_ada, b_ada[:, None, :])
    mod4 = mod.reshape(depth, N_MOD_ROWS, 1, 6 * D_MODEL)

    bd = _block_diag_mean()
    rope_tabs = _rope_tables(dec_seq)
    w_up_bf = w_up.astype(BF16)
    w_down_bf = w_down.astype(BF16)
    tile_w = lambda w, n: jnp.tile(w, n)[None, :]

    xp = x_prompt.reshape(1, batch * seq, D_MODEL)
    xs = x_sample
    new_k_a, new_v_a, new_k_b, new_v_b = [], [], [], []
    for i in range(depth):
        j = i // 2
        nw_attn = norm_attn_w[i][None, :]
        nw_ffn = norm_ffn_w[i][None, :]
        if i % 2 == 0:
            nkv = N_KV_A * HEAD_DIM
            w_qkv = w_qkv_a[j].astype(BF16)
            w_o = w_o_a[j].astype(BF16)
            qn, kn = tile_w(q_norm_a[j], N_HEADS), tile_w(k_norm_a[j], N_KV_A)
            q, k, v, kd, vd = _qkv_call(xp, mod4, i, 0, nw_attn, w_qkv, qn, kn, bd, None,
                                        nkv=nkv, dup=True, f32_kv=True, tm=512)
            new_k_a.append(k.reshape(batch, seq, N_KV_A, HEAD_DIM))
            new_v_a.append(v.reshape(batch, seq, N_KV_A, HEAD_DIM))
            shp = (batch, seq, -1)
            op = _ctx_attn_call(q.reshape(shp), kd.reshape(shp), vd.reshape(shp), sink_a[j])
            q, kd, vd = _qkv_call(xs, mod4, i, 1, nw_attn, w_qkv, qn, kn, bd, rope_tabs,
                                  nkv=nkv, dup=True, f32_kv=False, tm=512)
            o_s = _win_attn_call(q, kd, vd,
                                 cache_k_a[:, j].reshape(dec_batch, -1, nkv),
                                 cache_v_a[:, j].reshape(dec_batch, -1, nkv), sink_a[j])
        else:
            nkv = nq
            w_qkv = w_qkv_b[j].astype(BF16)
            w_o = w_o_b[j].astype(BF16)
            qn, kn = tile_w(q_norm_b[j], N_HEADS), tile_w(k_norm_b[j], N_HEADS)
            q, k, v = _qkv_call(xp, mod4, i, 0, nw_attn, w_qkv, qn, kn, bd, None,
                                nkv=nkv, dup=False, f32_kv=True, tm=512)
            new_k_b.append(k.reshape(batch, seq, N_HEADS, HEAD_DIM))
            new_v_b.append(v.reshape(batch, seq, N_HEADS, HEAD_DIM))
            shp = (batch, seq, -1)
            op = _ctx_attn_call(q.reshape(shp), k.reshape(shp), v.reshape(shp), None)
            q, k, v = _qkv_call(xs, mod4, i, 1, nw_attn, w_qkv, qn, kn, bd, None,
                                nkv=nkv, dup=False, f32_kv=False, tm=512)
            o_s = _nbr_attn_call(q, k, v, _nbr_bias_table(rpb_b[j], dec_seq // GRID_W),
                                 cache_k_b[:, j].reshape(dec_batch, -1, nq).astype(BF16),
                                 cache_v_b[:, j].reshape(dec_batch, -1, nq).astype(BF16))
        blk_w = (w_o, nw_ffn, w_up_bf, conv_w, conv_b[:, None, :], w_down_bf)
        xp = _block_ffn_call(xp, op.reshape(xp.shape), mod4, i, 0, *blk_w, seq_len=seq, tm=seq)
        xs = _block_ffn_call(xs, o_s, mod4, i, 1, *blk_w, seq_len=dec_seq, tm=512)
    return (xp.reshape(batch, seq, D_MODEL), xs,
            jnp.stack(new_k_a, axis=1), jnp.stack(new_v_a, axis=1),
            jnp.stack(new_k_b, axis=1), jnp.stack(new_v_b, axis=1))
```
